```python
import math
import jax, jax.numpy as jnp
from jax import lax
import numpy as np

D_MODEL = 4096
BATCH = 8
SEQ = 2048
DEPTH = 4

PLE_DIM = 256
Q_BLOCK = 128

A_HEADS = 16
A_HEAD_DIM = 128
IDX_HEADS = 16
IDX_DIM = 128
IDX_TOPK_MAX = 256
B_HEADS = 8
B_QK_DIM = 256
B_V_DIM = 256
RET_CHUNK = 128
C_GROUPS = 16
C_GROUP_DIM = 128
C_CHUNK = 128
D_HEADS = 16
D_NOPE = 128
D_ROPE = 64
D_V = 128
D_Q_LORA = 1536
D_KV_LORA = 512
ROPE_BASE = 10000.0

A_WIDTH = A_HEADS * A_HEAD_DIM
B_WIDTH = B_HEADS * B_V_DIM
C_WIDTH = C_GROUPS * C_GROUP_DIM
D_WIDTH = D_HEADS * D_V
D_QK = D_NOPE + D_ROPE

EVEN_SPLITS = (A_WIDTH, A_HEAD_DIM, A_HEAD_DIM, IDX_HEADS * IDX_DIM, IDX_DIM, IDX_HEADS, A_WIDTH,
               B_HEADS * B_QK_DIM, B_HEADS * B_QK_DIM, B_WIDTH, B_WIDTH)
EVEN_IN = sum(EVEN_SPLITS)
ODD_SPLITS = (C_WIDTH, C_WIDTH, C_WIDTH, D_Q_LORA, D_KV_LORA, D_ROPE, D_WIDTH)
ODD_IN = sum(ODD_SPLITS)
MIX_EVEN = A_WIDTH + B_WIDTH
MIX_ODD = C_WIDTH + D_WIDTH

kernel_name = "hybrid_dsa_retention_gmlp_mla_trunk"


def _split(z, sizes):
    cuts = [int(c) for c in np.cumsum(sizes)[:-1]]
    return jnp.split(z, cuts, axis=-1)


def rms_norm(x, g, eps=1e-6):
    xf = x.astype(jnp.float32)
    y = xf * lax.rsqrt(jnp.mean(xf * xf, axis=-1, keepdims=True) + eps)
    return (y * g.astype(jnp.float32)).astype(x.dtype)


def layer_norm(x, g, eps=1e-6):
    xf = x.astype(jnp.float32)
    xc = xf - jnp.mean(xf, axis=-1, keepdims=True)
    y = xc * lax.rsqrt(jnp.mean(xc * xc, axis=-1, keepdims=True) + eps)
    return (y * g.astype(jnp.float32)).astype(x.dtype)


def _to_blocks(a, nb):
    return jnp.moveaxis(a.reshape((a.shape[0], nb, Q_BLOCK) + a.shape[2:]), 1, 0)


def _from_blocks(a):
    a = jnp.moveaxis(a, 0, 1)
    return a.reshape(a.shape[0], a.shape[1] * a.shape[2], -1)


def dsa_attention(q, k, v, iq, ik, iw, pos):
    Bn, S = q.shape[0], q.shape[1]
    topk = min(IDX_TOPK_MAX, S // 4)
    nb = S // Q_BLOCK
    slopes = jnp.asarray(2.0 ** (-8.0 * np.arange(1, A_HEADS + 1) / A_HEADS), dtype=jnp.float32)
    t_blocks = jnp.arange(S, dtype=jnp.int32).reshape(nb, Q_BLOCK)
    key_idx = jnp.arange(S, dtype=jnp.int32)
    b_idx = jnp.arange(Bn)[:, None, None]
    scale = A_HEAD_DIM ** -0.5
    iscale = IDX_DIM ** -0.5
    wscale = IDX_HEADS ** -0.5

    def one_block(args):
        qb, iqb, iwb, posb, tb = args
        rel = jax.nn.relu(jnp.einsum('bqhd,bsd->bqhs', iqb, ik) * iscale)
        score = jnp.einsum('bqhs,bqh->bqs', rel, iwb * wscale).astype(jnp.float32)
        causal = key_idx[None, :] <= tb[:, None]
        score = jnp.where(causal[None], score, -jnp.inf)
        _, sel = lax.top_k(score, topk)
        ks = k[b_idx, sel]
        vs = v[b_idx, sel]
        ps = pos[b_idx, sel]
        valid = sel <= tb[None, :, None]
        logits = jnp.einsum('bqhd,bqkd->bhqk', qb, ks).astype(jnp.float32) * scale
        dist = (posb[:, :, None] - ps).astype(jnp.float32)
        logits = logits - slopes[None, :, None, None] * dist[:, None]
        logits = jnp.where(valid[:, None], logits, -jnp.inf)
        probs = jax.nn.softmax(logits, axis=-1).astype(v.dtype)
        return jnp.einsum('bhqk,bqkd->bqhd', probs, vs)

    out = lax.map(one_block, (_to_blocks(q, nb), _to_blocks(iq, nb), _to_blocks(iw, nb),
                              _to_blocks(pos, nb), t_blocks))
    return _from_blocks(out)


def retention(q, k, v, gn_gain):
    Bn, S, H, DK = q.shape
    DV = v.shape[-1]
    nc = S // RET_CHUNK
    dt = q.dtype
    log_gamma = jnp.log1p(-(2.0 ** (-5.0 - jnp.arange(H, dtype=jnp.float32))))
    n = jnp.arange(RET_CHUNK, dtype=jnp.float32)
    diff = n[:, None] - n[None, :]
    inner_decay = jnp.where(diff[None] >= 0,
                            jnp.exp(log_gamma[:, None, None] * jnp.maximum(diff, 0.0)[None]), 0.0)
    q_decay = jnp.exp(log_gamma[None, :] * (n[:, None] + 1.0))
    k_decay = jnp.exp(log_gamma[None, :] * (RET_CHUNK - 1.0 - n[:, None]))
    chunk_decay = jnp.exp(log_gamma * RET_CHUNK)
    k = k * (DK ** -0.5)

    def chunks(a):
        return jnp.moveaxis(a.reshape((Bn, nc, RET_CHUNK) + a.shape[2:]), 1, 0)

    def step(state, inp):
        qc, kc, vc = inp
        attn = jnp.einsum('bihd,bjhd->bhij', qc, kc) * inner_decay.astype(dt)[None]
        inner = jnp.einsum('bhij,bjhe->bihe', attn, vc)
        cross = jnp.einsum('bihd,bhde->bihe', qc, state.astype(dt)) * q_decay.astype(dt)[None, :, :, None]
        kv = jnp.einsum('bjhd,bjhe->bhde', kc * k_decay.astype(dt)[None, :, :, None], vc)
        state = state * chunk_decay[None, :, None, None] + kv.astype(jnp.float32)
        return state, inner + cross

    state0 = jnp.zeros((Bn, H, DK, DV), jnp.float32)
    _, out = lax.scan(step, state0, (chunks(q), chunks(k), chunks(v)))
    out = jnp.moveaxis(out, 0, 1).reshape(Bn, S, H, DV)
    out = layer_norm(out, gn_gain)
    return out.reshape(Bn, S, H * DV)


def chunked_spatial_gating(u, v, w_s, b_s, v_gain):
    Bn, S, _ = u.shape
    nc = S // C_CHUNK
    u = jax.nn.gelu(u)
    v = layer_norm(jax.nn.gelu(v), v_gain)
    vg = v.reshape(Bn, nc, C_CHUNK, C_GROUPS, C_GROUP_DIM)
    causal = jnp.tril(jnp.ones((C_CHUNK, C_CHUNK), dtype=bool))
    w = jnp.where(causal[None], w_s, jnp.zeros_like(w_s)).astype(v.dtype)
    mixed = jnp.einsum('gts,bnsgc->bntgc', w, vg) + jnp.swapaxes(b_s, 0, 1).astype(v.dtype)[None, None, :, :, None]
    return (u.reshape(vg.shape) * mixed).reshape(Bn, S, C_WIDTH)


def rope(x, pos):
    half = x.shape[-1] // 2
    inv = ROPE_BASE ** (-jnp.arange(half, dtype=jnp.float32) / half)
    ang = pos.astype(jnp.float32)[..., None] * inv
    cos = jnp.cos(ang)[:, :, None, :]
    sin = jnp.sin(ang)[:, :, None, :]
    x1 = x[..., :half].astype(jnp.float32)
    x2 = x[..., half:].astype(jnp.float32)
    return jnp.concatenate([x1 * cos - x2 * sin, x1 * sin + x2 * cos], axis=-1).astype(x.dtype)


def causal_block_attention(q, k, v, scale):
    S = q.shape[1]
    nb = S // Q_BLOCK
    t_blocks = jnp.arange(S, dtype=jnp.int32).reshape(nb, Q_BLOCK)
    key_idx = jnp.arange(S, dtype=jnp.int32)

    def one_block(args):
        qb, tb = args
        logits = jnp.einsum('bqhd,bshd->bhqs', qb, k).astype(jnp.float32) * scale
        mask = key_idx[None, :] <= tb[:, None]
        logits = jnp.where(mask[None, None], logits, -jnp.inf)
        probs = jax.nn.softmax(logits, axis=-1).astype(v.dtype)
        return jnp.einsum('bhqs,bshe->bqhe', probs, v)

    out = lax.map(one_block, (_to_blocks(q, nb), t_blocks))
    return _from_blocks(out)


def mla_attention(cq, ckv, krope, pos, q_lora_g, kv_lora_g, w_uq, w_ukv, qn_g, kn_g):
    Bn, S, _ = cq.shape
    q = (rms_norm(cq, q_lora_g) @ w_uq).reshape(Bn, S, D_HEADS, D_QK)
    kv = (rms_norm(ckv, kv_lora_g) @ w_ukv).reshape(Bn, S, D_HEADS, D_NOPE + D_V)
    k_nope, v = kv[..., :D_NOPE], kv[..., D_NOPE:]
    k = jnp.concatenate([k_nope, jnp.broadcast_to(krope[:, :, None, :], (Bn, S, D_HEADS, D_ROPE))], axis=-1)
    q = rms_norm(q, qn_g)
    k = rms_norm(k, kn_g)
    q = jnp.concatenate([q[..., :D_NOPE], rope(q[..., D_NOPE:], pos)], axis=-1)
    k = jnp.concatenate([k[..., :D_NOPE], rope(k[..., D_NOPE:], pos)], axis=-1)
    return causal_block_attention(q, k, v, D_QK ** -0.5)


def even_mixer(h, pos, w_in, a_qn, a_kn, b_gn, w_out):
    Bn, S, _ = h.shape
    z = h @ w_in
    qa, ka, va, iq, ik, iw, ga, qb, kb, vb, gb = _split(z, EVEN_SPLITS)
    qa = rms_norm(qa.reshape(Bn, S, A_HEADS, A_HEAD_DIM), a_qn)
    ka = rms_norm(ka, a_kn)
    a_out = dsa_attention(qa, ka, va, iq.reshape(Bn, S, IDX_HEADS, IDX_DIM), ik, iw, pos) * jax.nn.silu(ga)
    b_out = retention(qb.reshape(Bn, S, B_HEADS, B_QK_DIM), kb.reshape(Bn, S, B_HEADS, B_QK_DIM),
                      vb.reshape(Bn, S, B_HEADS, B_V_DIM), b_gn) * jax.nn.silu(gb)
    return jnp.concatenate([a_out, b_out], axis=-1) @ w_out


def odd_mixer(h, pos, w_in, c_vn, c_ws, c_bs, d_qlg, d_kvlg, d_wuq, d_wukv, d_qn, d_kn, w_out):
    z = h @ w_in
    cu, cv, gc, cq, ckv, krope, gd = _split(z, ODD_SPLITS)
    c_out = chunked_spatial_gating(cu, cv, c_ws, c_bs, c_vn) * jax.nn.silu(gc)
    d_out = mla_attention(cq, ckv, krope, pos, d_qlg, d_kvlg, d_wuq, d_wukv, d_qn, d_kn) * jax.nn.silu(gd)
    return jnp.concatenate([c_out, d_out], axis=-1) @ w_out


def setup_inputs(seed: int = 0) -> dict:
    key = jax.random.key(seed)
    ks = iter(jax.random.split(key, 32))
    n_even = (DEPTH + 1) // 2
    n_odd = DEPTH // 2
    f32 = jnp.float32

    def nrm(shape, scale):
        return jax.random.normal(next(ks), shape, f32) * scale

    def gain(shape, noise=0.02):
        return 1.0 + nrm(shape, noise)

    x = nrm((BATCH, SEQ, D_MODEL), 1.0)
    p = nrm((DEPTH, BATCH, SEQ, PLE_DIM), 1.0)
    offsets = jax.random.randint(next(ks), (BATCH, 1), 0, 4096, dtype=jnp.int32)
    positions = offsets + jnp.arange(SEQ, dtype=jnp.int32)[None, :]
    return {
        "x": x,
        "p": p,
        "positions": positions,
        "norm_in": gain((DEPTH, D_MODEL)),
        "even_w_in": nrm((n_even, D_MODEL, EVEN_IN), D_MODEL ** -0.5),
        "even_a_q_norm": gain((n_even, A_HEAD_DIM)),
        "even_a_k_norm": gain((n_even, A_HEAD_DIM)),
        "even_b_group_norm": gain((n_even, B_HEADS, B_V_DIM)),
        "even_w_out": nrm((n_even, MIX_EVEN, D_MODEL), MIX_EVEN ** -0.5),
        "odd_w_in": nrm((n_odd, D_MODEL, ODD_IN), D_MODEL ** -0.5),
        "odd_c_v_norm": gain((n_odd, C_WIDTH)),
        "odd_c_w_s": nrm((n_odd, C_GROUPS, C_CHUNK, C_CHUNK), 0.5 * C_CHUNK ** -0.5),
        "odd_c_b_s": gain((n_odd, C_GROUPS, C_CHUNK), 0.1),
        "odd_d_q_lora_norm": gain((n_odd, D_Q_LORA)),
        "odd_d_kv_lora_norm": gain((n_odd, D_KV_LORA)),
        "odd_d_w_uq": nrm((n_odd, D_Q_LORA, D_HEADS * D_QK), D_Q_LORA ** -0.5),
        "odd_d_w_ukv": nrm((n_odd, D_KV_LORA, D_HEADS * (D_NOPE + D_V)), D_KV_LORA ** -0.5),
        "odd_d_q_norm": gain((n_odd, D_QK)),
        "odd_d_k_norm": gain((n_odd, D_QK)),
        "odd_w_out": nrm((n_odd, MIX_ODD, D_MODEL), MIX_ODD ** -0.5),
        "ple_w_in": nrm((DEPTH, PLE_DIM, D_MODEL), PLE_DIM ** -0.5),
        "ple_w_gate": nrm((DEPTH, D_MODEL, D_MODEL), D_MODEL ** -0.5),
    }


def reference(x, p, positions, norm_in, even_w_in, even_a_q_norm, even_a_k_norm, even_b_group_norm,
              even_w_out, odd_w_in, odd_c_v_norm, odd_c_w_s, odd_c_b_s, odd_d_q_lora_norm,
              odd_d_kv_lora_norm, odd_d_w_uq, odd_d_w_ukv, odd_d_q_norm, odd_d_k_norm, odd_w_out,
              ple_w_in, ple_w_gate):
    for i in range(DEPTH):
        j = i // 2
        h = rms_norm(x, norm_in[i])
        if i % 2 == 0:
            y = even_mixer(h, positions, even_w_in[j], even_a_q_norm[j], even_a_k_norm[j],
                           even_b_group_norm[j], even_w_out[j])
        else:
            y = odd_mixer(h, positions, odd_w_in[j], odd_c_v_norm[j], odd_c_w_s[j], odd_c_b_s[j],
                          odd_d_q_lora_norm[j], odd_d_kv_lora_norm[j], odd_d_w_uq[j], odd_d_w_ukv[j],
                          odd_d_q_norm[j], odd_d_k_norm[j], odd_w_out[j])
        x = x + y
        x = x + (p[i] @ ple_w_in[i]) * jax.nn.sigmoid(x @ ple_w_gate[i])
    return x
```

```python
import functools
import math

import numpy as np
import jax
import jax.numpy as jnp
from jax import lax
from jax.experimental import pallas as pl
from jax.experimental.pallas import tpu as pltpu

F32 = jnp.float32
BF16 = jnp.bfloat16

D_MODEL = 4096
PLE_DIM = 256
A_HEADS = 16
A_HEAD_DIM = 128
IDX_HEADS = 16
IDX_DIM = 128
IDX_TOPK_MAX = 256
B_HEADS = 8
B_QK_DIM = 256
B_V_DIM = 256
RET_CHUNK = 128
C_GROUPS = 16
C_GROUP_DIM = 128
C_CHUNK = 128
D_HEADS = 16
D_NOPE = 128
D_ROPE = 64
D_V = 128
D_Q_LORA = 1536
D_KV_LORA = 512
ROPE_BASE = 10000.0
A_WIDTH = A_HEADS * A_HEAD_DIM
B_WIDTH = B_HEADS * B_V_DIM
C_WIDTH = C_GROUPS * C_GROUP_DIM
D_WIDTH = D_HEADS * D_V
D_QK = D_NOPE + D_ROPE
EVEN_SPLITS = (A_WIDTH, A_HEAD_DIM, A_HEAD_DIM, IDX_HEADS * IDX_DIM, IDX_DIM, IDX_HEADS, A_WIDTH,
               B_HEADS * B_QK_DIM, B_HEADS * B_QK_DIM, B_WIDTH, B_WIDTH)
ODD_SPLITS = (C_WIDTH, C_WIDTH, C_WIDTH, D_Q_LORA, D_KV_LORA, D_ROPE, D_WIDTH)

LANE = 128
D_HEAD_PAD = 256
VMEM_LIMIT = 56 * 1024 * 1024
NORM_EPS = 1e-6
MASK_BIAS = 1e30
NEG_BIG = -1e30
INT_MIN = np.int32(-2 ** 31)

EVEN_ORDER = ("qb", "kb", "vb", "gb", "qa", "iq", "ga", "ka", "va", "ik", "iw")
EVEN_NAMES = ("qa", "ka", "va", "iq", "ik", "iw", "ga", "qb", "kb", "vb", "gb")
EVEN_IW_PAD = LANE - IDX_HEADS
EVEN_OUT = sum(EVEN_SPLITS) + EVEN_IW_PAD
ODD_MAIN = sum(ODD_SPLITS) - D_ROPE


def _params(sem, vmem=None):
    return pltpu.CompilerParams(dimension_semantics=sem, vmem_limit_bytes=vmem)


def _dot(a, b):
    return jnp.dot(a, b, preferred_element_type=F32)


def _dot_nt(a, b):
    return lax.dot_general(a, b, (((1,), (1,)), ((), ())), preferred_element_type=F32)


def _dot_tn(a, b):
    return lax.dot_general(a, b, (((0,), (0,)), ((), ())), preferred_element_type=F32)


def _silu(x):
    return x * (1.0 / (1.0 + jnp.exp(-x)))


def _proj_kernel(x_ref, w_ref, o_ref):
    o_ref[...] = _dot(x_ref[...], w_ref[...]).astype(o_ref.dtype)


def _proj(x, w, tm, tn, out_dtype=BF16, name="proj"):
    m, k = x.shape
    n = w.shape[1]
    return pl.pallas_call(
        _proj_kernel,
        out_shape=jax.ShapeDtypeStruct((m, n), out_dtype),
        grid=(m // tm, n // tn),
        in_specs=[pl.BlockSpec((tm, k), lambda i, j: (i, 0)),
                  pl.BlockSpec((k, tn), lambda i, j: (0, j))],
        out_specs=pl.BlockSpec((tm, tn), lambda i, j: (i, j)),
        compiler_params=_params(("arbitrary", "arbitrary"), VMEM_LIMIT),
        name=name,
    )(x, w)


def _out_proj_kernel(a_ref, b_ref, wa_ref, wb_ref, x_ref, o_ref, ob_ref):
    y = x_ref[...] + _dot(a_ref[...], wa_ref[...]) + _dot(b_ref[...], wb_ref[...])
    o_ref[...] = y
    ob_ref[...] = y.astype(BF16)


def _out_proj(a, b, w, x, tm, tn):
    m, kh = a.shape
    n = w.shape[1]
    return pl.pallas_call(
        _out_proj_kernel,
        out_shape=(jax.ShapeDtypeStruct((m, n), F32), jax.ShapeDtypeStruct((m, n), BF16)),
        grid=(m // tm, n // tn),
        in_specs=[pl.BlockSpec((tm, kh), lambda i, j: (i, 0)),
                  pl.BlockSpec((tm, kh), lambda i, j: (i, 0)),
                  pl.BlockSpec((kh, tn), lambda i, j: (0, j)),
                  pl.BlockSpec((kh, tn), lambda i, j: (1, j)),
                  pl.BlockSpec((tm, tn), lambda i, j: (i, j))],
        out_specs=(pl.BlockSpec((tm, tn), lambda i, j: (i, j)),
                   pl.BlockSpec((tm, tn), lambda i, j: (i, j))),
        compiler_params=_params(("arbitrary", "arbitrary"), VMEM_LIMIT),
        name="out_proj",
    )(a, b, w, w, x)


def _ple_kernel(xb_ref, wg_ref, p_ref, wp_ref, x_ref, o_ref):
    gate = _dot(xb_ref[...], wg_ref[...])
    emb = _dot(p_ref[...].astype(BF16), wp_ref[...])
    o_ref[...] = x_ref[...] + emb * (1.0 / (1.0 + jnp.exp(-gate)))


def _ple(xb, wg, p, wp, x, tm, tn):
    m, d = xb.shape
    n = wg.shape[1]
    pd = p.shape[1]
    return pl.pallas_call(
        _ple_kernel,
        out_shape=jax.ShapeDtypeStruct((m, n), F32),
        grid=(m // tm, n // tn),
        in_specs=[pl.BlockSpec((tm, d), lambda i, j: (i, 0)),
                  pl.BlockSpec((d, tn), lambda i, j: (0, j)),
                  pl.BlockSpec((tm, pd), lambda i, j: (i, 0)),
                  pl.BlockSpec((pd, tn), lambda i, j: (0, j)),
                  pl.BlockSpec((tm, tn), lambda i, j: (i, j))],
        out_specs=pl.BlockSpec((tm, tn), lambda i, j: (i, j)),
        compiler_params=_params(("arbitrary", "arbitrary"), VMEM_LIMIT),
        name="ple",
    )(xb, wg, p, wp, x)


def _rmsnorm_kernel(x_ref, g_ref, o_ref):
    x = x_ref[...].astype(F32)
    r = lax.rsqrt(jnp.mean(x * x, axis=-1, keepdims=True) + NORM_EPS)
    o_ref[...] = (x * r * g_ref[...]).astype(o_ref.dtype)


def _rmsnorm(x, g, width, col_block, tm):
    m = x.shape[0]
    return pl.pallas_call(
        _rmsnorm_kernel,
        out_shape=jax.ShapeDtypeStruct((m, width), BF16),
        grid=(m // tm,),
        in_specs=[pl.BlockSpec((tm, width), lambda i: (i, col_block)),
                  pl.BlockSpec((1, width), lambda i: (0, 0))],
        out_specs=pl.BlockSpec((tm, width), lambda i: (i, 0)),
        compiler_params=_params(("arbitrary",), VMEM_LIMIT),
        name="rmsnorm",
    )(x, g)


DSA_TQ = 256
DSA_KC = 256


def _dsa_kernel(slopes_ref, qa_ref, iq_ref, ga_ref, iw_ref, posq_ref, ka_ref, va_ref, ik_ref, posk_ref,
                qg_ref, kg_ref, out_ref, kn_scr, qn_scr, key_scr, bias_scr, o_scr, j_scr, *, topk, seq):
    tq, kc = DSA_TQ, DSA_KC
    qi = pl.program_id(1)
    nkc = qi + 1

    @pl.when(qi == 0)
    def _():
        k = ka_ref[...].astype(F32)
        r = lax.rsqrt(jnp.mean(k * k, axis=-1, keepdims=True) + NORM_EPS)
        kn_scr[...] = (k * r * kg_ref[...]).astype(BF16)

    for h in range(A_HEADS):
        q = qa_ref[:, h * LANE:(h + 1) * LANE].astype(F32)
        r = lax.rsqrt(jnp.mean(q * q, axis=-1, keepdims=True) + NORM_EPS)
        qn_scr[h] = (q * r * (qg_ref[...] * (A_HEAD_DIM ** -0.5))).astype(BF16)

    row = qi * tq + lax.broadcasted_iota(jnp.int32, (tq, kc), 0)
    col0 = lax.broadcasted_iota(jnp.int32, (tq, kc), 1)

    wcol = iw_ref[...].astype(F32) * ((IDX_HEADS ** -0.5) * (IDX_DIM ** -0.5))

    def idx_body(c, carry):
        ikc = ik_ref[pl.ds(pl.multiple_of(c * kc, kc), kc), :]
        acc = jnp.zeros((tq, kc), F32)
        for h in range(IDX_HEADS):
            d = _dot_nt(iq_ref[:, h * LANE:(h + 1) * LANE], ikc)
            acc = acc + jnp.maximum(d, 0.0) * wcol[:, h:h + 1]
        bits = lax.bitcast_convert_type(acc, jnp.int32)
        key = bits ^ ((bits >> 31) & jnp.int32(0x7FFFFFFF))
        key_scr[c] = jnp.where(c * kc + col0 <= row, key, INT_MIN)
        return carry

    lax.fori_loop(0, nkc, idx_body, 0)

    def count_ge(thr):
        def body(c, cnt):
            return cnt + jnp.where(key_scr[c] >= thr, 1.0, 0.0)
        cnt = lax.fori_loop(0, nkc, body, jnp.zeros((tq, kc), F32))
        return jnp.sum(cnt, axis=1, keepdims=True)

    def bit_body(i, res):
        cand = res | jnp.left_shift(jnp.int32(1), 31 - i)
        total = count_ge(cand ^ INT_MIN)
        return jnp.where(total >= topk, cand, res)

    res = lax.fori_loop(0, 32, bit_body, jnp.zeros((tq, 1), jnp.int32))
    thr = res ^ INT_MIN
    c_ge = count_ge(thr)
    c_gt = c_ge - _count_eq(key_scr, thr, nkc, tq, kc)
    need_eq = topk - c_gt
    tie = jnp.logical_and(c_ge > topk, thr != INT_MIN)
    j_scr[...] = jnp.full((tq, 1), seq, jnp.int32)

    @pl.when(jnp.max(jnp.where(tie, 1.0, 0.0)) > 0.0)
    def _():
        def jbit_body(i, j):
            cand = j | jnp.left_shift(jnp.int32(1), (seq.bit_length() - 1) - i)

            def body(c, cnt):
                hit = jnp.logical_and(key_scr[c] == thr, c * kc + col0 < cand)
                return cnt + jnp.where(hit, 1.0, 0.0)
            cnt = lax.fori_loop(0, nkc, body, jnp.zeros((tq, kc), F32))
            below = jnp.sum(cnt, axis=1, keepdims=True)
            return jnp.where(below < need_eq, cand, j)

        j = lax.fori_loop(0, seq.bit_length(), jbit_body, jnp.zeros((tq, 1), jnp.int32))
        j_scr[...] = jnp.where(tie, j, seq)

    jmax = j_scr[...]
    posq = posq_ref[...]

    def bias_body(c, carry):
        key = key_scr[c]
        col = c * kc + col0
        keep = jnp.logical_or(key > thr, jnp.logical_and(key == thr, col <= jmax))
        keep = jnp.logical_and(keep, col <= row)
        bias_scr[c] = jnp.where(keep, posq - posk_ref[c], MASK_BIAS)
        return carry

    lax.fori_loop(0, nkc, bias_body, 0)

    def head_body(h, carry):
        q = qn_scr[h]
        slope = slopes_ref[h]

        def body(c, st):
            m, l, acc = st
            start = pl.multiple_of(c * kc, kc)
            s = _dot_nt(q, kn_scr[pl.ds(start, kc), :]) - slope * bias_scr[c]
            m_new = jnp.maximum(m, jnp.max(s, axis=1, keepdims=True))
            alpha = jnp.exp(m - m_new)
            p = jnp.exp(s - m_new)
            l = alpha * l + jnp.sum(p, axis=1, keepdims=True)
            acc = alpha * acc + _dot(p.astype(BF16), va_ref[pl.ds(start, kc), :])
            return m_new, l, acc

        st0 = (jnp.full((tq, 1), NEG_BIG, F32), jnp.zeros((tq, 1), F32), jnp.zeros((tq, A_HEAD_DIM), F32))
        _, l, acc = lax.fori_loop(0, nkc, body, st0)
        o_scr[h] = acc / l
        return carry

    lax.fori_loop(0, A_HEADS, head_body, 0)

    for h in range(A_HEADS):
        g = ga_ref[:, h * LANE:(h + 1) * LANE].astype(F32)
        out_ref[:, h * LANE:(h + 1) * LANE] = (o_scr[h] * _silu(g)).astype(BF16)


def _count_eq(key_scr, thr, nkc, tq, kc):
    def body(c, cnt):
        return cnt + jnp.where(key_scr[c] == thr, 1.0, 0.0)
    cnt = lax.fori_loop(0, nkc, body, jnp.zeros((tq, kc), F32))
    return jnp.sum(cnt, axis=1, keepdims=True)


def _dsa(z, pos_f, qg, kg, batch, seq):
    t = batch * seq
    tq, kc = DSA_TQ, DSA_KC
    nq = seq // tq
    nc = seq // kc
    topk = min(IDX_TOPK_MAX, seq // 4)
    slopes = jnp.asarray(2.0 ** (-8.0 * np.arange(1, A_HEADS + 1) / A_HEADS), dtype=F32)
    posq = pos_f.reshape(t, 1)
    posk = pos_f.reshape(batch, nc, 1, kc)
    wide = lambda blk: pl.BlockSpec((tq, A_WIDTH), lambda b, i: (b * nq + i, blk))
    keys = lambda blk: pl.BlockSpec((seq, LANE), lambda b, i: (b, blk))
    kern = functools.partial(_dsa_kernel, topk=topk, seq=seq)
    return pl.pallas_call(
        kern,
        out_shape=jax.ShapeDtypeStruct((t, A_WIDTH), BF16),
        grid=(batch, nq),
        in_specs=[pl.BlockSpec(memory_space=pltpu.SMEM),
                  wide(4), wide(5), wide(6),
                  pl.BlockSpec((tq, LANE), lambda b, i: (b * nq + i, 115)),
                  pl.BlockSpec((tq, 1), lambda b, i: (b * nq + i, 0)),
                  keys(112), keys(113), keys(114),
                  pl.BlockSpec((None, nc, 1, kc), lambda b, i: (b, 0, 0, 0)),
                  pl.BlockSpec((1, LANE), lambda b, i: (0, 0)),
                  pl.BlockSpec((1, LANE), lambda b, i: (0, 0))],
        out_specs=pl.BlockSpec((tq, A_WIDTH), lambda b, i: (b * nq + i, 0)),
        scratch_shapes=[pltpu.VMEM((seq, LANE), BF16),
                        pltpu.VMEM((A_HEADS, tq, LANE), BF16),
                        pltpu.VMEM((nc, tq, kc), jnp.int32),
                        pltpu.VMEM((nc, tq, kc), F32),
                        pltpu.VMEM((A_HEADS, tq, LANE), F32),
                        pltpu.VMEM((tq, 1), jnp.int32)],
        compiler_params=_params(("arbitrary", "arbitrary"), VMEM_LIMIT),
        name="dsa",
    )(slopes, z, z, z, z, posq, z, z, z, posk, qg, kg)


def _retention_kernel(cdec_ref, q_ref, k_ref, v_ref, g_ref, inner_ref, qdec_ref, kdec_ref, gn_ref,
                      out_ref, state_scr, *, seq):
    c = RET_CHUNK
    h = pl.program_id(1)
    state_scr[...] = jnp.zeros_like(state_scr)
    chunk_decay = cdec_ref[h]
    inner = inner_ref[...]
    qdec = qdec_ref[...]
    kdec = kdec_ref[...]
    gn = gn_ref[...]

    def body(i, carry):
        rows = pl.ds(pl.multiple_of(i * c, c), c)
        q = q_ref[rows, :]
        k = k_ref[rows, :]
        v = v_ref[rows, :]
        attn = _dot_nt(q, k) * inner
        state = state_scr[...]
        o = _dot(attn.astype(BF16), v) + _dot(q, state.astype(BF16)) * qdec
        kd = (k.astype(F32) * kdec).astype(BF16)
        state_scr[...] = state * chunk_decay + _dot_tn(kd, v)
        oc = o - jnp.mean(o, axis=-1, keepdims=True)
        y = oc * lax.rsqrt(jnp.mean(oc * oc, axis=-1, keepdims=True) + NORM_EPS) * gn
        out_ref[rows, :] = (y * _silu(g_ref[rows, :].astype(F32))).astype(BF16)
        return carry

    lax.fori_loop(0, seq // c, body, 0)


def _retention(z, gn, batch, seq):
    t = batch * seq
    c = RET_CHUNK
    hs = np.arange(B_HEADS, dtype=np.float64)
    log_gamma = np.log1p(-(2.0 ** (-5.0 - hs)))
    n = np.arange(c, dtype=np.float64)
    diff = n[:, None] - n[None, :]
    scale = B_QK_DIM ** -0.5
    inner = np.where(diff[None] >= 0, np.exp(log_gamma[:, None, None] * np.maximum(diff, 0.0)[None]), 0.0) * scale
    qdec = np.exp(log_gamma[:, None] * (n[None, :] + 1.0))[:, :, None]
    kdec = np.exp(log_gamma[:, None] * (c - 1.0 - n[None, :]))[:, :, None] * scale
    cdec = np.exp(log_gamma * c)
    nb = B_HEADS
    blk = lambda off: pl.BlockSpec((seq, B_V_DIM), lambda b, h: (b, off + h))
    tab = lambda w: pl.BlockSpec((None, c, w), lambda b, h: (h, 0, 0))
    return pl.pallas_call(
        functools.partial(_retention_kernel, seq=seq),
        out_shape=jax.ShapeDtypeStruct((t, B_WIDTH), BF16),
        grid=(batch, B_HEADS),
        in_specs=[pl.BlockSpec(memory_space=pltpu.SMEM),
                  blk(0), blk(nb), blk(2 * nb), blk(3 * nb),
                  tab(c), tab(1), tab(1),
                  pl.BlockSpec((None, 1, B_V_DIM), lambda b, h: (h, 0, 0))],
        out_specs=pl.BlockSpec((seq, B_V_DIM), lambda b, h: (b, h)),
        scratch_shapes=[pltpu.VMEM((B_QK_DIM, B_V_DIM), F32)],
        compiler_params=_params(("arbitrary", "arbitrary"), VMEM_LIMIT),
        name="retention",
    )(jnp.asarray(cdec, F32), z, z, z, z, jnp.asarray(inner, F32), jnp.asarray(qdec, F32),
      jnp.asarray(kdec, F32), gn.reshape(B_HEADS, 1, B_V_DIM))


GMLP_TM = 256


def _gmlp_kernel(u_ref, v_ref, g_ref, vg_ref, ws_ref, bst_ref, out_ref):
    c = C_CHUNK
    tril = lax.broadcasted_iota(jnp.int32, (c, c), 0) >= lax.broadcasted_iota(jnp.int32, (c, c), 1)
    v = jax.nn.gelu(v_ref[...].astype(F32))
    vc = v - jnp.mean(v, axis=-1, keepdims=True)
    vn = (vc * lax.rsqrt(jnp.mean(vc * vc, axis=-1, keepdims=True) + NORM_EPS) * vg_ref[...]).astype(BF16)
    bst = bst_ref[...]
    for g in range(C_GROUPS):
        w = jnp.where(tril, ws_ref[g], 0.0).astype(BF16)
        cols = slice(g * C_GROUP_DIM, (g + 1) * C_GROUP_DIM)
        for j in range(GMLP_TM // c):
            rows = slice(j * c, (j + 1) * c)
            mixed = _dot(w, vn[rows, cols]) + bst[:, g:g + 1]
            u = jax.nn.gelu(u_ref[rows, cols].astype(F32))
            out_ref[rows, cols] = (u * mixed * _silu(g_ref[rows, cols].astype(F32))).astype(BF16)


def _gmlp(z, v_gain, w_s, b_s):
    t = z.shape[0]
    tm = GMLP_TM
    bst = jnp.pad(b_s.T, ((0, 0), (0, LANE - C_GROUPS)))
    blk = lambda j: pl.BlockSpec((tm, C_WIDTH), lambda i: (i, j))
    return pl.pallas_call(
        _gmlp_kernel,
        out_shape=jax.ShapeDtypeStruct((t, C_WIDTH), BF16),
        grid=(t // tm,),
        in_specs=[blk(0), blk(1), blk(2),
                  pl.BlockSpec((1, C_WIDTH), lambda i: (0, 0)),
                  pl.BlockSpec((C_GROUPS, C_CHUNK, C_CHUNK), lambda i: (0, 0, 0)),
                  pl.BlockSpec((C_CHUNK, LANE), lambda i: (0, 0))],
        out_specs=pl.BlockSpec((tm, C_WIDTH), lambda i: (i, 0)),
        compiler_params=_params(("arbitrary",), VMEM_LIMIT),
        name="gmlp",
    )(z, z, z, v_gain.reshape(1, C_WIDTH), w_s, bst)


MLA_TM = 256
MLA_TQ = 256
MLA_KC = 256
ROPE_HALF = D_ROPE // 2


def _rope_layout(v64):
    z = jnp.zeros(v64.shape[:-1] + (ROPE_HALF,), v64.dtype)
    return jnp.concatenate([v64[..., :ROPE_HALF], z, v64[..., ROPE_HALF:], z], axis=-1)


def _mla_prep_kernel(q_ref, kn_ref, kr_ref, pos_ref, tab_ref, qg_ref, kg_ref, qo_ref, ko_ref):
    tab = tab_ref[...]
    ang = pos_ref[...] * tab[0:1, :]
    cos = jnp.cos(ang) * tab[1:2, :]
    sin = jnp.sin(ang) * tab[2:3, :]

    def rope(x):
        return x * cos + pltpu.roll(x, LANE // 2, 1) * sin

    qg = qg_ref[...]
    kg = kg_ref[...]
    scale = D_QK ** -0.5
    kr = kr_ref[...].astype(F32)
    kr_ss = jnp.sum(kr * kr, axis=-1, keepdims=True)
    kr_rot = rope(kr * kg[:, LANE:])
    for h in range(D_HEADS):
        lo = slice(h * D_HEAD_PAD, h * D_HEAD_PAD + LANE)
        hi = slice(h * D_HEAD_PAD + LANE, (h + 1) * D_HEAD_PAD)
        q1 = q_ref[:, lo].astype(F32)
        q2 = q_ref[:, hi].astype(F32)
        ss = jnp.sum(q1 * q1, axis=-1, keepdims=True) + jnp.sum(q2 * q2, axis=-1, keepdims=True)
        r = lax.rsqrt(ss * (1.0 / D_QK) + NORM_EPS) * scale
        qo_ref[:, lo] = (q1 * r * qg[:, :LANE]).astype(BF16)
        qo_ref[:, hi] = (rope(q2 * qg[:, LANE:]) * r).astype(BF16)
        k1 = kn_ref[:, h * LANE:(h + 1) * LANE].astype(F32)
        ss = jnp.sum(k1 * k1, axis=-1, keepdims=True) + kr_ss
        r = lax.rsqrt(ss * (1.0 / D_QK) + NORM_EPS)
        ko_ref[:, lo] = (k1 * r * kg[:, :LANE]).astype(BF16)
        ko_ref[:, hi] = (kr_rot * r).astype(BF16)


def _mla_prep(q_raw, kv, kr, pos_f, qn_g, kn_g):
    t = q_raw.shape[0]
    tm = MLA_TM
    inv = ROPE_BASE ** (-np.arange(ROPE_HALF, dtype=np.float64) / ROPE_HALF)
    zero = np.zeros(ROPE_HALF)
    one = np.ones(ROPE_HALF)
    tab = np.zeros((8, LANE))
    tab[0] = np.concatenate([inv, zero, inv, zero])
    tab[1] = np.concatenate([one, zero, one, zero])
    tab[2] = np.concatenate([-one, zero, one, zero])
    lay = lambda g: jnp.concatenate([g[:D_NOPE], _rope_layout(g[D_NOPE:])]).reshape(1, D_HEAD_PAD)
    width = D_HEADS * D_HEAD_PAD
    return pl.pallas_call(
        _mla_prep_kernel,
        out_shape=(jax.ShapeDtypeStruct((t, width), BF16), jax.ShapeDtypeStruct((t, width), BF16)),
        grid=(t // tm,),
        in_specs=[pl.BlockSpec((tm, width), lambda i: (i, 0)),
                  pl.BlockSpec((tm, D_HEADS * D_NOPE), lambda i: (i, 0)),
                  pl.BlockSpec((tm, LANE), lambda i: (i, 0)),
                  pl.BlockSpec((tm, 1), lambda i: (i, 0)),
                  pl.BlockSpec((8, LANE), lambda i: (0, 0)),
                  pl.BlockSpec((1, D_HEAD_PAD), lambda i: (0, 0)),
                  pl.BlockSpec((1, D_HEAD_PAD), lambda i: (0, 0))],
        out_specs=(pl.BlockSpec((tm, width), lambda i: (i, 0)),
                   pl.BlockSpec((tm, width), lambda i: (i, 0))),
        compiler_params=_params(("arbitrary",), VMEM_LIMIT),
        name="mla_prep",
    )(q_raw, kv, kr, pos_f.reshape(t, 1), jnp.asarray(tab, F32), lay(qn_g), lay(kn_g))


def _mla_attn_kernel(q_ref, k_ref, v_ref, g_ref, out_ref):
    tq, kc = MLA_TQ, MLA_KC
    qi = pl.program_id(2)
    q = q_ref[...]

    def step(c, st, diag):
        m, l, acc = st
        start = pl.multiple_of(c * kc, kc)
        s = _dot_nt(q, k_ref[pl.ds(start, kc), :])
        if diag:
            keep = lax.broadcasted_iota(jnp.int32, (tq, kc), 1) <= lax.broadcasted_iota(jnp.int32, (tq, kc), 0)
            s = jnp.where(keep, s, NEG_BIG)
        m_new = jnp.maximum(m, jnp.max(s, axis=1, keepdims=True))
        alpha = jnp.exp(m - m_new)
        p = jnp.exp(s - m_new)
        l = alpha * l + jnp.sum(p, axis=1, keepdims=True)
        acc = alpha * acc + _dot(p.astype(BF16), v_ref[pl.ds(start, kc), :])
        return m_new, l, acc

    st = (jnp.full((tq, 1), NEG_BIG, F32), jnp.zeros((tq, 1), F32), jnp.zeros((tq, D_V), F32))
    st = lax.fori_loop(0, qi, lambda c, s: step(c, s, False), st)
    _, l, acc = step(qi, st, True)
    out_ref[...] = (acc / l * _silu(g_ref[...].astype(F32))).astype(BF16)


def _mla_attn(q, k, kv, z, batch, seq):
    t = batch * seq
    tq = MLA_TQ
    nq = seq // tq
    gd0 = (ODD_MAIN - D_WIDTH) // D_V
    return pl.pallas_call(
        _mla_attn_kernel,
        out_shape=jax.ShapeDtypeStruct((t, D_WIDTH), BF16),
        grid=(batch, D_HEADS, nq),
        in_specs=[pl.BlockSpec((tq, D_HEAD_PAD), lambda b, h, i: (b * nq + i, h)),
                  pl.BlockSpec((seq, D_HEAD_PAD), lambda b, h, i: (b, h)),
                  pl.BlockSpec((seq, D_V), lambda b, h, i: (b, D_HEADS * D_NOPE // D_V + h)),
                  pl.BlockSpec((tq, D_V), lambda b, h, i: (b * nq + i, gd0 + h))],
        out_specs=pl.BlockSpec((tq, D_V), lambda b, h, i: (b * nq + i, h)),
        compiler_params=_params(("arbitrary", "arbitrary", "arbitrary"), VMEM_LIMIT),
        name="mla_attn",
    )(q, k, kv, z)


def _even_w_in(w):
    parts = dict(zip(EVEN_NAMES, jnp.split(w, [int(c) for c in np.cumsum(EVEN_SPLITS)[:-1]], axis=1)))
    cols = [parts[n] for n in EVEN_ORDER] + [jnp.zeros((w.shape[0], EVEN_IW_PAD), w.dtype)]
    return jnp.concatenate(cols, axis=1).astype(BF16)


def _odd_w_in(w):
    cu, cv, gc, cq, ckv, krope, gd = jnp.split(w, [int(c) for c in np.cumsum(ODD_SPLITS)[:-1]], axis=1)
    main = jnp.concatenate([cu, cv, gc, cq, ckv, gd], axis=1).astype(BF16)
    return main, _rope_layout(krope).astype(BF16)


def _w_uq_layout(w):
    w = w.reshape(D_Q_LORA, D_HEADS, D_QK)
    w = jnp.concatenate([w[..., :D_NOPE], _rope_layout(w[..., D_NOPE:])], axis=-1)
    return w.reshape(D_Q_LORA, D_HEADS * D_HEAD_PAD).astype(BF16)


def _w_ukv_layout(w):
    w = w.reshape(D_KV_LORA, D_HEADS, D_NOPE + D_V)
    return jnp.concatenate([w[..., :D_NOPE].reshape(D_KV_LORA, -1), w[..., D_NOPE:].reshape(D_KV_LORA, -1)],
                           axis=1).astype(BF16)


def _even_mixer(h, pos_f, w_in, a_qn, a_kn, b_gn, batch, seq):
    z = _proj(h, _even_w_in(w_in), 1024, 512, name="even_in")
    a = _dsa(z, pos_f, a_qn.reshape(1, LANE), a_kn.reshape(1, LANE), batch, seq)
    b = _retention(z, b_gn, batch, seq)
    return a, b


def _odd_mixer(h, pos_f, w_in, c_vn, c_ws, c_bs, d_qlg, d_kvlg, d_wuq, d_wukv, d_qn, d_kn, batch, seq):
    w_main, w_kr = _odd_w_in(w_in)
    z = _proj(h, w_main, 1024, 1024, name="odd_in")
    kr = _proj(h, w_kr, 1024, LANE, name="odd_in_rope")
    c = _gmlp(z, c_vn, c_ws, c_bs)
    cq = _rmsnorm(z, d_qlg.reshape(1, D_Q_LORA), D_Q_LORA, 3 * C_WIDTH // D_Q_LORA, 512)
    ckv = _rmsnorm(z, d_kvlg.reshape(1, D_KV_LORA), D_KV_LORA, (3 * C_WIDTH + D_Q_LORA) // D_KV_LORA, 512)
    q_raw = _proj(cq, _w_uq_layout(d_wuq), 1024, 1024, name="mla_uq")
    kv = _proj(ckv, _w_ukv_layout(d_wukv), 1024, 1024, name="mla_ukv")
    q, k = _mla_prep(q_raw, kv, kr, pos_f, d_qn, d_kn)
    d = _mla_attn(q, k, kv, z, batch, seq)
    return c, d


def kernel(x, p, positions, norm_in, even_w_in, even_a_q_norm, even_a_k_norm, even_b_group_norm, even_w_out, odd_w_in, odd_c_v_norm, odd_c_w_s, odd_c_b_s, odd_d_q_lora_norm, odd_d_kv_lora_norm, odd_d_w_uq, odd_d_w_ukv, odd_d_q_norm, odd_d_k_norm, odd_w_out, ple_w_in, ple_w_gate):
    batch, seq, d = x.shape
    t = batch * seq
    depth = norm_in.shape[0]
    xs = x.reshape(t, d)
    pos_f = positions.astype(F32)
    for i in range(depth):
        j = i // 2
        h = _rmsnorm(xs, norm_in[i].reshape(1, d), d, 0, 256)
        if i % 2 == 0:
            m1, m2 = _even_mixer(h, pos_f, even_w_in[j], even_a_q_norm[j], even_a_k_norm[j],
                                 even_b_group_norm[j], batch, seq)
            w_out = even_w_out[j]
        else:
            m1, m2 = _odd_mixer(h, pos_f, odd_w_in[j], odd_c_v_norm[j], odd_c_w_s[j], odd_c_b_s[j],
                                odd_d_q_lora_norm[j], odd_d_kv_lora_norm[j], odd_d_w_uq[j], odd_d_w_ukv[j],
                                odd_d_q_norm[j], odd_d_k_norm[j], batch, seq)
            w_out = odd_w_out[j]
        xs, xb = _out_proj(m1, m2, w_out.astype(BF16), xs, 1024, 512)
        xs = _ple(xb, ple_w_gate[i].astype(BF16), p[i].reshape(t, PLE_DIM), ple_w_in[i].astype(BF16), xs, 1024, 512)
    return xs.reshape(batch, seq, d)
```

```python
import functools
import math

import numpy as np
import jax
import jax.numpy as jnp
from jax import lax
from jax.experimental import pallas as pl
from jax.experimental.pallas import tpu as pltpu

F32 = jnp.float32
BF16 = jnp.bfloat16

D_MODEL = 4096
PLE_DIM = 256
A_HEADS = 16
A_HEAD_DIM = 128
IDX_HEADS = 16
IDX_DIM = 128
IDX_TOPK_MAX = 256
B_HEADS = 8
B_QK_DIM = 256
B_V_DIM = 256
RET_CHUNK = 128
C_GROUPS = 16
C_GROUP_DIM = 128
C_CHUNK = 128
D_HEADS = 16
D_NOPE = 128
D_ROPE = 64
D_V = 128
D_Q_LORA = 1536
D_KV_LORA = 512
ROPE_BASE = 10000.0
A_WIDTH = A_HEADS * A_HEAD_DIM
B_WIDTH = B_HEADS * B_V_DIM
C_WIDTH = C_GROUPS * C_GROUP_DIM
D_WIDTH = D_HEADS * D_V
D_QK = D_NOPE + D_ROPE
EVEN_SPLITS = (A_WIDTH, A_HEAD_DIM, A_HEAD_DIM, IDX_HEADS * IDX_DIM, IDX_DIM, IDX_HEADS, A_WIDTH,
               B_HEADS * B_QK_DIM, B_HEADS * B_QK_DIM, B_WIDTH, B_WIDTH)
ODD_SPLITS = (C_WIDTH, C_WIDTH, C_WIDTH, D_Q_LORA, D_KV_LORA, D_ROPE, D_WIDTH)

LANE = 128
D_HEAD_PAD = 256
VMEM_LIMIT = 56 * 1024 * 1024
NORM_EPS = 1e-6
MASK_BIAS = 1e30
NEG_BIG = -1e30
INT_MIN = np.int32(-2 ** 31)

EVEN_ORDER = ("qb", "kb", "vb", "gb", "qa", "iq", "ga", "ka", "va", "ik", "iw")
EVEN_NAMES = ("qa", "ka", "va", "iq", "ik", "iw", "ga", "qb", "kb", "vb", "gb")
EVEN_IW_PAD = LANE - IDX_HEADS
EVEN_OUT = sum(EVEN_SPLITS) + EVEN_IW_PAD
ODD_MAIN = sum(ODD_SPLITS) - D_ROPE


def _params(sem, vmem=None):
    return pltpu.CompilerParams(dimension_semantics=sem, vmem_limit_bytes=vmem)


def _dot(a, b):
    return jnp.dot(a, b, preferred_element_type=F32)


def _dot_nt(a, b):
    return lax.dot_general(a, b, (((1,), (1,)), ((), ())), preferred_element_type=F32)


def _dot_tn(a, b):
    return lax.dot_general(a, b, (((0,), (0,)), ((), ())), preferred_element_type=F32)


def _silu(x):
    return x * (1.0 / (1.0 + jnp.exp(-x)))


def _proj_kernel(x_ref, w_ref, o_ref):
    o_ref[...] = _dot(x_ref[...], w_ref[...]).astype(o_ref.dtype)


def _proj(x, w, tm, tn, out_dtype=BF16, name="proj"):
    m, k = x.shape
    n = w.shape[1]
    return pl.pallas_call(
        _proj_kernel,
        out_shape=jax.ShapeDtypeStruct((m, n), out_dtype),
        grid=(m // tm, n // tn),
        in_specs=[pl.BlockSpec((tm, k), lambda i, j: (i, 0)),
                  pl.BlockSpec((k, tn), lambda i, j: (0, j))],
        out_specs=pl.BlockSpec((tm, tn), lambda i, j: (i, j)),
        compiler_params=_params(("arbitrary", "arbitrary"), VMEM_LIMIT),
        name=name,
    )(x, w)


def _out_proj_kernel(a_ref, b_ref, wa_ref, wb_ref, x_ref, o_ref, ob_ref):
    y = x_ref[...] + _dot(a_ref[...], wa_ref[...]) + _dot(b_ref[...], wb_ref[...])
    o_ref[...] = y
    ob_ref[...] = y.astype(BF16)


def _out_proj(a, b, w, x, tm, tn):
    m, kh = a.shape
    n = w.shape[1]
    return pl.pallas_call(
        _out_proj_kernel,
        out_shape=(jax.ShapeDtypeStruct((m, n), F32), jax.ShapeDtypeStruct((m, n), BF16)),
        grid=(m // tm, n // tn),
        in_specs=[pl.BlockSpec((tm, kh), lambda i, j: (i, 0)),
                  pl.BlockSpec((tm, kh), lambda i, j: (i, 0)),
                  pl.BlockSpec((kh, tn), lambda i, j: (0, j)),
                  pl.BlockSpec((kh, tn), lambda i, j: (1, j)),
                  pl.BlockSpec((tm, tn), lambda i, j: (i, j))],
        out_specs=(pl.BlockSpec((tm, tn), lambda i, j: (i, j)),
                   pl.BlockSpec((tm, tn), lambda i, j: (i, j))),
        compiler_params=_params(("arbitrary", "arbitrary"), VMEM_LIMIT),
        name="out_proj",
    )(a, b, w, w, x)


def _ple_kernel(xb_ref, wg_ref, p_ref, wp_ref, x_ref, o_ref):
    gate = _dot(xb_ref[...], wg_ref[...])
    emb = _dot(p_ref[...].astype(BF16), wp_ref[...])
    o_ref[...] = x_ref[...] + emb * (1.0 / (1.0 + jnp.exp(-gate)))


def _ple(xb, wg, p, wp, x, tm, tn):
    m, d = xb.shape
    n = wg.shape[1]
    pd = p.shape[1]
    return pl.pallas_call(
        _ple_kernel,
        out_shape=jax.ShapeDtypeStruct((m, n), F32),
        grid=(m // tm, n // tn),
        in_specs=[pl.BlockSpec((tm, d), lambda i, j: (i, 0)),
                  pl.BlockSpec((d, tn), lambda i, j: (0, j)),
                  pl.BlockSpec((tm, pd), lambda i, j: (i, 0)),
                  pl.BlockSpec((pd, tn), lambda i, j: (0, j)),
                  pl.BlockSpec((tm, tn), lambda i, j: (i, j))],
        out_specs=pl.BlockSpec((tm, tn), lambda i, j: (i, j)),
        compiler_params=_params(("arbitrary", "arbitrary"), VMEM_LIMIT),
        name="ple",
    )(xb, wg, p, wp, x)


def _rmsnorm_kernel(x_ref, g_ref, o_ref):
    x = x_ref[...].astype(F32)
    r = lax.rsqrt(jnp.mean(x * x, axis=-1, keepdims=True) + NORM_EPS)
    o_ref[...] = (x * r * g_ref[...]).astype(o_ref.dtype)


def _rmsnorm(x, g, width, col_block, tm):
    m = x.shape[0]
    return pl.pallas_call(
        _rmsnorm_kernel,
        out_shape=jax.ShapeDtypeStruct((m, width), BF16),
        grid=(m // tm,),
        in_specs=[pl.BlockSpec((tm, width), lambda i: (i, col_block)),
                  pl.BlockSpec((1, width), lambda i: (0, 0))],
        out_specs=pl.BlockSpec((tm, width), lambda i: (i, 0)),
        compiler_params=_params(("arbitrary",), VMEM_LIMIT),
        name="rmsnorm",
    )(x, g)


DSA_TQ = 256
DSA_KC = 128
SUBLANES = 8
ALIBI_SLOPES = tuple(float(2.0 ** (-8.0 * (h + 1) / A_HEADS)) for h in range(A_HEADS))


def _dsa_kernel(qa_ref, iq_ref, ga_ref, iw_ref, posq_ref, ka_ref, va_ref, ik_ref, posk_ref, qg_ref, kg_ref,
                out_ref, kn_scr, qn_scr, key_scr, bias_scr, m_scr, l_scr, acc_scr, j_scr, *, topk, seq):
    tq, kc = DSA_TQ, DSA_KC
    qi = pl.program_id(1)
    nkc = (qi + 1) * (tq // kc)

    @pl.when(qi == 0)
    def _():
        k = ka_ref[...].astype(F32)
        r = lax.rsqrt(jnp.mean(k * k, axis=-1, keepdims=True) + NORM_EPS)
        kn_scr[...] = (k * r * kg_ref[...]).astype(BF16)

    for h in range(A_HEADS):
        q = qa_ref[:, h * LANE:(h + 1) * LANE].astype(F32)
        r = lax.rsqrt(jnp.mean(q * q, axis=-1, keepdims=True) + NORM_EPS)
        qn_scr[h] = (q * r * (qg_ref[...] * (A_HEAD_DIM ** -0.5))).astype(BF16)

    kidx0 = lax.broadcasted_iota(jnp.int32, (kc, tq), 0)
    qidx = qi * tq + lax.broadcasted_iota(jnp.int32, (kc, tq), 1)

    w_t = (iw_ref[...].astype(F32) * ((IDX_HEADS ** -0.5) * (IDX_DIM ** -0.5))).T

    def idx_body(c, carry):
        ikc = ik_ref[pl.ds(pl.multiple_of(c * tq, tq), tq), :]
        acc = jnp.zeros((tq, tq), F32)
        for h in range(IDX_HEADS):
            d = _dot_nt(ikc, iq_ref[:, h * LANE:(h + 1) * LANE])
            acc = acc + jnp.maximum(d, 0.0) * w_t[h:h + 1, :]
        bits = lax.bitcast_convert_type(acc, jnp.int32)
        key = bits ^ ((bits >> 31) & jnp.int32(0x7FFFFFFF))
        for j in range(tq // kc):
            cj = c * (tq // kc) + j
            key_scr[cj] = jnp.where(cj * kc + kidx0 <= qidx, key[j * kc:(j + 1) * kc], INT_MIN)
        return carry

    lax.fori_loop(0, qi + 1, idx_body, 0)

    def count(pred):
        def body(c, cnt):
            hit = jnp.where(pred(key_scr[c], c), 1.0, 0.0)
            return cnt + jnp.sum(hit.reshape(kc // SUBLANES, SUBLANES, tq), axis=0)
        cnt = lax.fori_loop(0, nkc, body, jnp.zeros((SUBLANES, tq), F32))
        return jnp.sum(cnt, axis=0, keepdims=True)

    def bit_body(i, res):
        cand = res | jnp.left_shift(jnp.int32(1), 31 - i)
        thr_c = cand ^ INT_MIN
        total = count(lambda key, c: key >= thr_c)
        return jnp.where(total >= topk, cand, res)

    res = lax.fori_loop(0, 32, bit_body, jnp.zeros((1, tq), jnp.int32))
    thr = res ^ INT_MIN
    c_ge = count(lambda key, c: key >= thr)
    c_gt = count(lambda key, c: key > thr)
    need_eq = topk - c_gt
    tie = jnp.logical_and(c_ge > topk, thr != INT_MIN)
    j_scr[...] = jnp.full((1, tq), seq, jnp.int32)

    @pl.when(jnp.max(jnp.where(tie, 1.0, 0.0)) > 0.0)
    def _():
        def jbit_body(i, j):
            cand = j | jnp.left_shift(jnp.int32(1), (seq.bit_length() - 1) - i)
            below = count(lambda key, c: jnp.logical_and(key == thr, c * kc + kidx0 < cand))
            return jnp.where(below < need_eq, cand, j)

        j = lax.fori_loop(0, seq.bit_length(), jbit_body, jnp.zeros((1, tq), jnp.int32))
        j_scr[...] = jnp.where(tie, j, seq)

    jmax = j_scr[...]
    posq = posq_ref[...]

    def bias_body(c, carry):
        key = key_scr[c]
        kid = c * kc + kidx0
        keep = jnp.logical_or(key > thr, jnp.logical_and(key == thr, kid <= jmax))
        keep = jnp.logical_and(keep, kid <= qidx)
        dist = posq - posk_ref[pl.ds(pl.multiple_of(c * kc, kc), kc), :]
        bias_scr[c] = jnp.where(keep, dist, MASK_BIAS)
        return carry

    lax.fori_loop(0, nkc, bias_body, 0)

    m_scr[...] = jnp.full(m_scr.shape, NEG_BIG, F32)
    l_scr[...] = jnp.zeros(l_scr.shape, F32)
    acc_scr[...] = jnp.zeros(acc_scr.shape, F32)

    def attn_body(c, carry):
        start = pl.multiple_of(c * kc, kc)
        kch = kn_scr[pl.ds(start, kc), :]
        vch = va_ref[pl.ds(start, kc), :]
        bias = bias_scr[c]
        qk_next = _dot_nt(kch, qn_scr[0])
        for h in range(A_HEADS):
            qk = qk_next
            if h + 1 < A_HEADS:
                qk_next = _dot_nt(kch, qn_scr[h + 1])
            s = qk - ALIBI_SLOPES[h] * bias
            m = m_scr[h]
            m_new = jnp.maximum(m, jnp.max(s, axis=0, keepdims=True))
            alpha = jnp.exp(m - m_new)
            p = jnp.exp(s - m_new)
            l_scr[h] = alpha * l_scr[h] + jnp.sum(p, axis=0, keepdims=True)
            acc_scr[h] = alpha * acc_scr[h] + _dot_tn(vch, p.astype(BF16))
            m_scr[h] = m_new
        return carry

    lax.fori_loop(0, nkc, attn_body, 0)

    for h in range(A_HEADS):
        g = ga_ref[:, h * LANE:(h + 1) * LANE].astype(F32)
        o = (acc_scr[h] / l_scr[h]).T
        out_ref[:, h * LANE:(h + 1) * LANE] = (o * _silu(g)).astype(BF16)


def _dsa(z, pos_f, qg, kg, batch, seq):
    t = batch * seq
    tq, kc = DSA_TQ, DSA_KC
    nq = seq // tq
    nc = seq // kc
    topk = min(IDX_TOPK_MAX, seq // 4)
    posq = pos_f.reshape(batch, nq, 1, tq)
    posk = pos_f.reshape(t, 1)
    wide = lambda blk: pl.BlockSpec((tq, A_WIDTH), lambda b, i: (b * nq + i, blk))
    keys = lambda blk: pl.BlockSpec((seq, LANE), lambda b, i: (b, blk))
    kern = functools.partial(_dsa_kernel, topk=topk, seq=seq)
    return pl.pallas_call(
        kern,
        out_shape=jax.ShapeDtypeStruct((t, A_WIDTH), BF16),
        grid=(batch, nq),
        in_specs=[wide(4), wide(5), wide(6),
                  pl.BlockSpec((tq, LANE), lambda b, i: (b * nq + i, 115)),
                  pl.BlockSpec((None, None, 1, tq), lambda b, i: (b, i, 0, 0)),
                  keys(112), keys(113), keys(114),
                  pl.BlockSpec((seq, 1), lambda b, i: (b, 0)),
                  pl.BlockSpec((1, LANE), lambda b, i: (0, 0)),
                  pl.BlockSpec((1, LANE), lambda b, i: (0, 0))],
        out_specs=pl.BlockSpec((tq, A_WIDTH), lambda b, i: (b * nq + i, 0)),
        scratch_shapes=[pltpu.VMEM((seq, LANE), BF16),
                        pltpu.VMEM((A_HEADS, tq, LANE), BF16),
                        pltpu.VMEM((nc, kc, tq), jnp.int32),
                        pltpu.VMEM((nc, kc, tq), F32),
                        pltpu.VMEM((A_HEADS, 1, tq), F32),
                        pltpu.VMEM((A_HEADS, 1, tq), F32),
                        pltpu.VMEM((A_HEADS, A_HEAD_DIM, tq), F32),
                        pltpu.VMEM((1, tq), jnp.int32)],
        compiler_params=_params(("arbitrary", "arbitrary"), VMEM_LIMIT),
        name="dsa",
    )(z, z, z, z, posq, z, z, z, posk, qg, kg)


def _retention_kernel(cdec_ref, q_ref, k_ref, v_ref, g_ref, inner_ref, qdec_ref, kdec_ref, gn_ref,
                      out_ref, state_scr, *, seq):
    c = RET_CHUNK
    h = pl.program_id(1)
    state_scr[...] = jnp.zeros_like(state_scr)
    chunk_decay = cdec_ref[h]
    inner = inner_ref[...]
    qdec = qdec_ref[...]
    kdec = kdec_ref[...]
    gn = gn_ref[...]

    def body(i, carry):
        rows = pl.ds(pl.multiple_of(i * c, c), c)
        q = q_ref[rows, :]
        k = k_ref[rows, :]
        v = v_ref[rows, :]
        attn = _dot_nt(q, k) * inner
        state = state_scr[...]
        o = _dot(attn.astype(BF16), v) + _dot(q, state.astype(BF16)) * qdec
        kd = (k.astype(F32) * kdec).astype(BF16)
        state_scr[...] = state * chunk_decay + _dot_tn(kd, v)
        oc = o - jnp.mean(o, axis=-1, keepdims=True)
        y = oc * lax.rsqrt(jnp.mean(oc * oc, axis=-1, keepdims=True) + NORM_EPS) * gn
        out_ref[rows, :] = (y * _silu(g_ref[rows, :].astype(F32))).astype(BF16)
        return carry

    lax.fori_loop(0, seq // c, body, 0)


def _retention(z, gn, batch, seq):
    t = batch * seq
    c = RET_CHUNK
    hs = np.arange(B_HEADS, dtype=np.float64)
    log_gamma = np.log1p(-(2.0 ** (-5.0 - hs)))
    n = np.arange(c, dtype=np.float64)
    diff = n[:, None] - n[None, :]
    scale = B_QK_DIM ** -0.5
    inner = np.where(diff[None] >= 0, np.exp(log_gamma[:, None, None] * np.maximum(diff, 0.0)[None]), 0.0) * scale
    qdec = np.exp(log_gamma[:, None] * (n[None, :] + 1.0))[:, :, None]
    kdec = np.exp(log_gamma[:, None] * (c - 1.0 - n[None, :]))[:, :, None] * scale
    cdec = np.exp(log_gamma * c)
    nb = B_HEADS
    blk = lambda off: pl.BlockSpec((seq, B_V_DIM), lambda b, h: (b, off + h))
    tab = lambda w: pl.BlockSpec((None, c, w), lambda b, h: (h, 0, 0))
    return pl.pallas_call(
        functools.partial(_retention_kernel, seq=seq),
        out_shape=jax.ShapeDtypeStruct((t, B_WIDTH), BF16),
        grid=(batch, B_HEADS),
        in_specs=[pl.BlockSpec(memory_space=pltpu.SMEM),
                  blk(0), blk(nb), blk(2 * nb), blk(3 * nb),
                  tab(c), tab(1), tab(1),
                  pl.BlockSpec((None, 1, B_V_DIM), lambda b, h: (h, 0, 0))],
        out_specs=pl.BlockSpec((seq, B_V_DIM), lambda b, h: (b, h)),
        scratch_shapes=[pltpu.VMEM((B_QK_DIM, B_V_DIM), F32)],
        compiler_params=_params(("arbitrary", "arbitrary"), VMEM_LIMIT),
        name="retention",
    )(jnp.asarray(cdec, F32), z, z, z, z, jnp.asarray(inner, F32), jnp.asarray(qdec, F32),
      jnp.asarray(kdec, F32), gn.reshape(B_HEADS, 1, B_V_DIM))


GMLP_TM = 256


def _gmlp_kernel(u_ref, v_ref, g_ref, vg_ref, ws_ref, bst_ref, out_ref):
    c = C_CHUNK
    tril = lax.broadcasted_iota(jnp.int32, (c, c), 0) >= lax.broadcasted_iota(jnp.int32, (c, c), 1)
    v = jax.nn.gelu(v_ref[...].astype(F32))
    vc = v - jnp.mean(v, axis=-1, keepdims=True)
    vn = (vc * lax.rsqrt(jnp.mean(vc * vc, axis=-1, keepdims=True) + NORM_EPS) * vg_ref[...]).astype(BF16)
    bst = bst_ref[...]
    for g in range(C_GROUPS):
        w = jnp.where(tril, ws_ref[g], 0.0).astype(BF16)
        cols = slice(g * C_GROUP_DIM, (g + 1) * C_GROUP_DIM)
        for j in range(GMLP_TM // c):
            rows = slice(j * c, (j + 1) * c)
            mixed = _dot(w, vn[rows, cols]) + bst[:, g:g + 1]
            u = jax.nn.gelu(u_ref[rows, cols].astype(F32))
            out_ref[rows, cols] = (u * mixed * _silu(g_ref[rows, cols].astype(F32))).astype(BF16)


def _gmlp(z, v_gain, w_s, b_s):
    t = z.shape[0]
    tm = GMLP_TM
    bst = jnp.pad(b_s.T, ((0, 0), (0, LANE - C_GROUPS)))
    blk = lambda j: pl.BlockSpec((tm, C_WIDTH), lambda i: (i, j))
    return pl.pallas_call(
        _gmlp_kernel,
        out_shape=jax.ShapeDtypeStruct((t, C_WIDTH), BF16),
        grid=(t // tm,),
        in_specs=[blk(0), blk(1), blk(2),
                  pl.BlockSpec((1, C_WIDTH), lambda i: (0, 0)),
                  pl.BlockSpec((C_GROUPS, C_CHUNK, C_CHUNK), lambda i: (0, 0, 0)),
                  pl.BlockSpec((C_CHUNK, LANE), lambda i: (0, 0))],
        out_specs=pl.BlockSpec((tm, C_WIDTH), lambda i: (i, 0)),
        compiler_params=_params(("arbitrary",), VMEM_LIMIT),
        name="gmlp",
    )(z, z, z, v_gain.reshape(1, C_WIDTH), w_s, bst)


MLA_TM = 256
MLA_TQ = 256
MLA_KC = 512
ROPE_HALF = D_ROPE // 2


def _rope_layout(v64):
    z = jnp.zeros(v64.shape[:-1] + (ROPE_HALF,), v64.dtype)
    return jnp.concatenate([v64[..., :ROPE_HALF], z, v64[..., ROPE_HALF:], z], axis=-1)


def _mla_prep_kernel(q_ref, kn_ref, kr_ref, pos_ref, tab_ref, qg_ref, kg_ref, qo_ref, ko_ref):
    tab = tab_ref[...]
    ang = pos_ref[...] * tab[0:1, :]
    cos = jnp.cos(ang) * tab[1:2, :]
    sin = jnp.sin(ang) * tab[2:3, :]

    def rope(x):
        return x * cos + pltpu.roll(x, LANE // 2, 1) * sin

    qg = qg_ref[...]
    kg = kg_ref[...]
    scale = D_QK ** -0.5
    kr = kr_ref[...].astype(F32)
    kr_ss = jnp.sum(kr * kr, axis=-1, keepdims=True)
    kr_rot = rope(kr * kg[:, LANE:])
    for h in range(D_HEADS):
        lo = slice(h * D_HEAD_PAD, h * D_HEAD_PAD + LANE)
        hi = slice(h * D_HEAD_PAD + LANE, (h + 1) * D_HEAD_PAD)
        q1 = q_ref[:, lo].astype(F32)
        q2 = q_ref[:, hi].astype(F32)
        ss = jnp.sum(q1 * q1, axis=-1, keepdims=True) + jnp.sum(q2 * q2, axis=-1, keepdims=True)
        r = lax.rsqrt(ss * (1.0 / D_QK) + NORM_EPS) * scale
        qo_ref[:, lo] = (q1 * r * qg[:, :LANE]).astype(BF16)
        qo_ref[:, hi] = (rope(q2 * qg[:, LANE:]) * r).astype(BF16)
        k1 = kn_ref[:, h * LANE:(h + 1) * LANE].astype(F32)
        ss = jnp.sum(k1 * k1, axis=-1, keepdims=True) + kr_ss
        r = lax.rsqrt(ss * (1.0 / D_QK) + NORM_EPS)
        ko_ref[:, lo] = (k1 * r * kg[:, :LANE]).astype(BF16)
        ko_ref[:, hi] = (kr_rot * r).astype(BF16)


def _mla_prep(q_raw, kv, kr, pos_f, qn_g, kn_g):
    t = q_raw.shape[0]
    tm = MLA_TM
    inv = ROPE_BASE ** (-np.arange(ROPE_HALF, dtype=np.float64) / ROPE_HALF)
    zero = np.zeros(ROPE_HALF)
    one = np.ones(ROPE_HALF)
    tab = np.zeros((8, LANE))
    tab[0] = np.concatenate([inv, zero, inv, zero])
    tab[1] = np.concatenate([one, zero, one, zero])
    tab[2] = np.concatenate([-one, zero, one, zero])
    lay = lambda g: jnp.concatenate([g[:D_NOPE], _rope_layout(g[D_NOPE:])]).reshape(1, D_HEAD_PAD)
    width = D_HEADS * D_HEAD_PAD
    return pl.pallas_call(
        _mla_prep_kernel,
        out_shape=(jax.ShapeDtypeStruct((t, width), BF16), jax.ShapeDtypeStruct((t, width), BF16)),
        grid=(t // tm,),
        in_specs=[pl.BlockSpec((tm, width), lambda i: (i, 0)),
                  pl.BlockSpec((tm, D_HEADS * D_NOPE), lambda i: (i, 0)),
                  pl.BlockSpec((tm, LANE), lambda i: (i, 0)),
                  pl.BlockSpec((tm, 1), lambda i: (i, 0)),
                  pl.BlockSpec((8, LANE), lambda i: (0, 0)),
                  pl.BlockSpec((1, D_HEAD_PAD), lambda i: (0, 0)),
                  pl.BlockSpec((1, D_HEAD_PAD), lambda i: (0, 0))],
        out_specs=(pl.BlockSpec((tm, width), lambda i: (i, 0)),
                   pl.BlockSpec((tm, width), lambda i: (i, 0))),
        compiler_params=_params(("arbitrary",), VMEM_LIMIT),
        name="mla_prep",
    )(q_raw, kv, kr, pos_f.reshape(t, 1), jnp.asarray(tab, F32), lay(qn_g), lay(kn_g))


MLA_HG = 4


def _mla_attn_kernel(q_ref, k_ref, v_ref, g_ref, out_ref, m_scr, l_scr, acc_scr):
    tq, kc = MLA_TQ, MLA_KC
    qi = pl.program_id(2)
    m_scr[...] = jnp.full(m_scr.shape, NEG_BIG, F32)
    l_scr[...] = jnp.zeros(l_scr.shape, F32)
    acc_scr[...] = jnp.zeros(acc_scr.shape, F32)

    def step(c, diag):
        start = pl.multiple_of(c * kc, kc)
        if diag:
            keep = (c * kc + lax.broadcasted_iota(jnp.int32, (kc, tq), 0)
                    <= qi * tq + lax.broadcasted_iota(jnp.int32, (kc, tq), 1))
        def logits(h):
            return _dot_nt(k_ref[pl.ds(start, kc), h * D_HEAD_PAD:(h + 1) * D_HEAD_PAD],
                           q_ref[:, h * D_HEAD_PAD:(h + 1) * D_HEAD_PAD])

        s_next = logits(0)
        for h in range(MLA_HG):
            s = s_next
            if h + 1 < MLA_HG:
                s_next = logits(h + 1)
            if diag:
                s = jnp.where(keep, s, NEG_BIG)
            m = m_scr[h]
            m_new = jnp.maximum(m, jnp.max(s, axis=0, keepdims=True))
            alpha = jnp.exp(m - m_new)
            p = jnp.exp(s - m_new)
            l_scr[h] = alpha * l_scr[h] + jnp.sum(p, axis=0, keepdims=True)
            acc_scr[h] = alpha * acc_scr[h] + _dot_tn(v_ref[pl.ds(start, kc), h * D_V:(h + 1) * D_V], p.astype(BF16))
            m_scr[h] = m_new

    def body(c, carry):
        step(c, False)
        return carry

    n_full = (qi * tq) // kc
    lax.fori_loop(0, n_full, body, 0)
    step(n_full, True)
    for h in range(MLA_HG):
        g = g_ref[:, h * D_V:(h + 1) * D_V].astype(F32)
        o = (acc_scr[h] / l_scr[h]).T
        out_ref[:, h * D_V:(h + 1) * D_V] = (o * _silu(g)).astype(BF16)


def _mla_attn(q, k, kv, z, batch, seq):
    assert MLA_KC % MLA_TQ == 0
    t = batch * seq
    tq = MLA_TQ
    nq = seq // tq
    hg = MLA_HG
    v0 = D_HEADS * D_NOPE // (hg * D_V)
    gd0 = (ODD_MAIN - D_WIDTH) // (hg * D_V)
    return pl.pallas_call(
        _mla_attn_kernel,
        out_shape=jax.ShapeDtypeStruct((t, D_WIDTH), BF16),
        grid=(batch, D_HEADS // hg, nq),
        in_specs=[pl.BlockSpec((tq, hg * D_HEAD_PAD), lambda b, h, i: (b * nq + i, h)),
                  pl.BlockSpec((seq, hg * D_HEAD_PAD), lambda b, h, i: (b, h)),
                  pl.BlockSpec((seq, hg * D_V), lambda b, h, i: (b, v0 + h)),
                  pl.BlockSpec((tq, hg * D_V), lambda b, h, i: (b * nq + i, gd0 + h))],
        out_specs=pl.BlockSpec((tq, hg * D_V), lambda b, h, i: (b * nq + i, h)),
        scratch_shapes=[pltpu.VMEM((hg, 1, tq), F32),
                        pltpu.VMEM((hg, 1, tq), F32),
                        pltpu.VMEM((hg, D_V, tq), F32)],
        compiler_params=_params(("arbitrary", "arbitrary", "arbitrary"), VMEM_LIMIT),
        name="mla_attn",
    )(q, k, kv, z)


def _even_w_in(w):
    parts = dict(zip(EVEN_NAMES, jnp.split(w, [int(c) for c in np.cumsum(EVEN_SPLITS)[:-1]], axis=1)))
    cols = [parts[n] for n in EVEN_ORDER] + [jnp.zeros((w.shape[0], EVEN_IW_PAD), w.dtype)]
    return jnp.concatenate(cols, axis=1).astype(BF16)


def _odd_w_in(w):
    cu, cv, gc, cq, ckv, krope, gd = jnp.split(w, [int(c) for c in np.cumsum(ODD_SPLITS)[:-1]], axis=1)
    main = jnp.concatenate([cu, cv, gc, cq, ckv, gd], axis=1).astype(BF16)
    return main, _rope_layout(krope).astype(BF16)


def _w_uq_layout(w):
    w = w.reshape(D_Q_LORA, D_HEADS, D_QK)
    w = jnp.concatenate([w[..., :D_NOPE], _rope_layout(w[..., D_NOPE:])], axis=-1)
    return w.reshape(D_Q_LORA, D_HEADS * D_HEAD_PAD).astype(BF16)


def _w_ukv_layout(w):
    w = w.reshape(D_KV_LORA, D_HEADS, D_NOPE + D_V)
    return jnp.concatenate([w[..., :D_NOPE].reshape(D_KV_LORA, -1), w[..., D_NOPE:].reshape(D_KV_LORA, -1)],
                           axis=1).astype(BF16)


def _even_mixer(h, pos_f, w_in, a_qn, a_kn, b_gn, batch, seq):
    z = _proj(h, _even_w_in(w_in), 1024, 512, name="even_in")
    a = _dsa(z, pos_f, a_qn.reshape(1, LANE), a_kn.reshape(1, LANE), batch, seq)
    b = _retention(z, b_gn, batch, seq)
    return a, b


def _odd_mixer(h, pos_f, w_in, c_vn, c_ws, c_bs, d_qlg, d_kvlg, d_wuq, d_wukv, d_qn, d_kn, batch, seq):
    w_main, w_kr = _odd_w_in(w_in)
    z = _proj(h, w_main, 1024, 1024, name="odd_in")
    kr = _proj(h, w_kr, 1024, LANE, name="odd_in_rope")
    c = _gmlp(z, c_vn, c_ws, c_bs)
    cq = _rmsnorm(z, d_qlg.reshape(1, D_Q_LORA), D_Q_LORA, 3 * C_WIDTH // D_Q_LORA, 512)
    ckv = _rmsnorm(z, d_kvlg.reshape(1, D_KV_LORA), D_KV_LORA, (3 * C_WIDTH + D_Q_LORA) // D_KV_LORA, 512)
    q_raw = _proj(cq, _w_uq_layout(d_wuq), 1024, 1024, name="mla_uq")
    kv = _proj(ckv, _w_ukv_layout(d_wukv), 1024, 1024, name="mla_ukv")
    q, k = _mla_prep(q_raw, kv, kr, pos_f, d_qn, d_kn)
    d = _mla_attn(q, k, kv, z, batch, seq)
    return c, d


def kernel(x, p, positions, norm_in, even_w_in, even_a_q_norm, even_a_k_norm, even_b_group_norm, even_w_out, odd_w_in, odd_c_v_norm, odd_c_w_s, odd_c_b_s, odd_d_q_lora_norm, odd_d_kv_lora_norm, odd_d_w_uq, odd_d_w_ukv, odd_d_q_norm, odd_d_k_norm, odd_w_out, ple_w_in, ple_w_gate):
    batch, seq, d = x.shape
    t = batch * seq
    depth = norm_in.shape[0]
    xs = x.reshape(t, d)
    pos_f = positions.astype(F32)
    for i in range(depth):
        j = i // 2
        h = _rmsnorm(xs, norm_in[i].reshape(1, d), d, 0, 256)
        if i % 2 == 0:
            m1, m2 = _even_mixer(h, pos_f, even_w_in[j], even_a_q_norm[j], even_a_k_norm[j],
                                 even_b_group_norm[j], batch, seq)
            w_out = even_w_out[j]
        else:
            m1, m2 = _odd_mixer(h, pos_f, odd_w_in[j], odd_c_v_norm[j], odd_c_w_s[j], odd_c_b_s[j],
                                odd_d_q_lora_norm[j], odd_d_kv_lora_norm[j], odd_d_w_uq[j], odd_d_w_ukv[j],
                                odd_d_q_norm[j], odd_d_k_norm[j], batch, seq)
            w_out = odd_w_out[j]
        xs, xb = _out_proj(m1, m2, w_out.astype(BF16), xs, 1024, 512)
        xs = _ple(xb, ple_w_gate[i].astype(BF16), p[i].reshape(t, PLE_DIM), ple_w_in[i].astype(BF16), xs, 1024, 512)
    return xs.reshape(batch, seq, d)
```

```python
import functools
import math

import numpy as np
import jax
import jax.numpy as jnp
from jax import lax
from jax.experimental import pallas as pl
from jax.experimental.pallas import tpu as pltpu

F32 = jnp.float32
BF16 = jnp.bfloat16

D_MODEL = 4096
PLE_DIM = 256
A_HEADS = 16
A_HEAD_DIM = 128
IDX_HEADS = 16
IDX_DIM = 128
IDX_TOPK_MAX = 256
B_HEADS = 8
B_QK_DIM = 256
B_V_DIM = 256
RET_CHUNK = 128
C_GROUPS = 16
C_GROUP_DIM = 128
C_CHUNK = 128
D_HEADS = 16
D_NOPE = 128
D_ROPE = 64
D_V = 128
D_Q_LORA = 1536
D_KV_LORA = 512
ROPE_BASE = 10000.0
A_WIDTH = A_HEADS * A_HEAD_DIM
B_WIDTH = B_HEADS * B_V_DIM
C_WIDTH = C_GROUPS * C_GROUP_DIM
D_WIDTH = D_HEADS * D_V
D_QK = D_NOPE + D_ROPE
EVEN_SPLITS = (A_WIDTH, A_HEAD_DIM, A_HEAD_DIM, IDX_HEADS * IDX_DIM, IDX_DIM, IDX_HEADS, A_WIDTH,
               B_HEADS * B_QK_DIM, B_HEADS * B_QK_DIM, B_WIDTH, B_WIDTH)
ODD_SPLITS = (C_WIDTH, C_WIDTH, C_WIDTH, D_Q_LORA, D_KV_LORA, D_ROPE, D_WIDTH)

LANE = 128
D_HEAD_PAD = 256
VMEM_LIMIT = 56 * 1024 * 1024
NORM_EPS = 1e-6
MASK_BIAS = 1e30
NEG_BIG = -1e30
INT_MIN = np.int32(-2 ** 31)

EVEN_ORDER = ("qb", "kb", "vb", "gb", "qa", "iq", "ga", "ka", "va", "ik", "iw")
EVEN_NAMES = ("qa", "ka", "va", "iq", "ik", "iw", "ga", "qb", "kb", "vb", "gb")
EVEN_IW_PAD = LANE - IDX_HEADS
EVEN_OUT = sum(EVEN_SPLITS) + EVEN_IW_PAD
ODD_MAIN = sum(ODD_SPLITS) - D_ROPE


def _params(sem, vmem=None):
    return pltpu.CompilerParams(dimension_semantics=sem, vmem_limit_bytes=vmem)


def _dot(a, b):
    return jnp.dot(a, b, preferred_element_type=F32)


def _dot_nt(a, b):
    return lax.dot_general(a, b, (((1,), (1,)), ((), ())), preferred_element_type=F32)


def _dot_tn(a, b):
    return lax.dot_general(a, b, (((0,), (0,)), ((), ())), preferred_element_type=F32)


def _silu(x):
    return x * (1.0 / (1.0 + jnp.exp(-x)))


def _norm_proj_kernel(x_ref, w_ref, o_ref, r_scr):
    @pl.when(pl.program_id(1) == 0)
    def _():
        x = x_ref[...].astype(F32)
        r_scr[...] = lax.rsqrt(jnp.mean(x * x, axis=-1, keepdims=True) + NORM_EPS)

    o_ref[...] = (_dot(x_ref[...], w_ref[...]) * r_scr[...]).astype(o_ref.dtype)


def _norm_proj(x, w, tm, tn, col_block=0, name="norm_proj"):
    m = x.shape[0]
    k, n = w.shape
    return pl.pallas_call(
        _norm_proj_kernel,
        out_shape=jax.ShapeDtypeStruct((m, n), BF16),
        grid=(m // tm, n // tn),
        in_specs=[pl.BlockSpec((tm, k), lambda i, j: (i, col_block)),
                  pl.BlockSpec((k, tn), lambda i, j: (0, j))],
        out_specs=pl.BlockSpec((tm, tn), lambda i, j: (i, j)),
        scratch_shapes=[pltpu.VMEM((tm, 1), F32)],
        compiler_params=_params(("arbitrary", "arbitrary"), VMEM_LIMIT),
        name=name,
    )(x, w)


def _out_proj_kernel(a_ref, b_ref, wa_ref, wb_ref, x_ref, o_ref, ob_ref):
    y = x_ref[...] + _dot(a_ref[...], wa_ref[...]) + _dot(b_ref[...], wb_ref[...])
    o_ref[...] = y
    ob_ref[...] = y.astype(BF16)


def _out_proj(a, b, w, layer, x, tm, tn):
    m, kh = a.shape
    n = w.shape[2]
    return pl.pallas_call(
        _out_proj_kernel,
        out_shape=(jax.ShapeDtypeStruct((m, n), F32), jax.ShapeDtypeStruct((m, n), BF16)),
        grid=(m // tm, n // tn),
        in_specs=[pl.BlockSpec((tm, kh), lambda i, j: (i, 0)),
                  pl.BlockSpec((tm, kh), lambda i, j: (i, 0)),
                  pl.BlockSpec((None, kh, tn), lambda i, j: (layer, 0, j)),
                  pl.BlockSpec((None, kh, tn), lambda i, j: (layer, 1, j)),
                  pl.BlockSpec((tm, tn), lambda i, j: (i, j))],
        out_specs=(pl.BlockSpec((tm, tn), lambda i, j: (i, j)),
                   pl.BlockSpec((tm, tn), lambda i, j: (i, j))),
        compiler_params=_params(("arbitrary", "arbitrary"), VMEM_LIMIT),
        name="out_proj",
    )(a, b, w, w, x)


def _ple_kernel(xb_ref, wg_ref, p_ref, wp_ref, x_ref, o_ref, ob_ref):
    gate = _dot(xb_ref[...], wg_ref[...])
    emb = _dot(p_ref[...].astype(BF16), wp_ref[...])
    y = x_ref[...] + emb * (1.0 / (1.0 + jnp.exp(-gate)))
    o_ref[...] = y
    ob_ref[...] = y.astype(BF16)


def _ple(xb, wg, p, wp, layer, x, tm, tn):
    m, d = xb.shape
    n = wg.shape[2]
    pd = p.shape[2]
    return pl.pallas_call(
        _ple_kernel,
        out_shape=(jax.ShapeDtypeStruct((m, n), F32), jax.ShapeDtypeStruct((m, n), BF16)),
        grid=(m // tm, n // tn),
        in_specs=[pl.BlockSpec((tm, d), lambda i, j: (i, 0)),
                  pl.BlockSpec((None, d, tn), lambda i, j: (layer, 0, j)),
                  pl.BlockSpec((None, tm, pd), lambda i, j: (layer, i, 0)),
                  pl.BlockSpec((None, pd, tn), lambda i, j: (layer, 0, j)),
                  pl.BlockSpec((tm, tn), lambda i, j: (i, j))],
        out_specs=(pl.BlockSpec((tm, tn), lambda i, j: (i, j)),
                   pl.BlockSpec((tm, tn), lambda i, j: (i, j))),
        compiler_params=_params(("arbitrary", "arbitrary"), VMEM_LIMIT),
        name="ple",
    )(xb, wg, p, wp, x)


DSA_TQ = 256
DSA_KC = 128
SUBLANES = 8
LOG2E = math.log2(math.e)
ALIBI_SLOPES_LOG2 = tuple(float(2.0 ** (-8.0 * (h + 1) / A_HEADS)) * LOG2E for h in range(A_HEADS))


def _dsa_kernel(qa_ref, iq_ref, ga_ref, iw_ref, posq_ref, ka_ref, va_ref, ik_ref, posk_ref, qg_ref, kg_ref,
                out_ref, kn_scr, qn_scr, key_scr, bias_scr, m_scr, l_scr, acc_scr, j_scr, *, topk, seq):
    tq, kc = DSA_TQ, DSA_KC
    qi = pl.program_id(1)
    nkc = (qi + 1) * (tq // kc)

    @pl.when(qi == 0)
    def _():
        k = ka_ref[...].astype(F32)
        r = lax.rsqrt(jnp.mean(k * k, axis=-1, keepdims=True) + NORM_EPS)
        kn_scr[...] = (k * r * kg_ref[...]).astype(BF16)

    for h in range(A_HEADS):
        q = qa_ref[:, h * LANE:(h + 1) * LANE].astype(F32)
        r = lax.rsqrt(jnp.mean(q * q, axis=-1, keepdims=True) + NORM_EPS)
        qn_scr[h] = (q * r * (qg_ref[...] * (A_HEAD_DIM ** -0.5 * LOG2E))).astype(BF16)

    kidx0 = lax.broadcasted_iota(jnp.int32, (kc, tq), 0)
    qidx = qi * tq + lax.broadcasted_iota(jnp.int32, (kc, tq), 1)

    w_t = (iw_ref[...].astype(F32) * ((IDX_HEADS ** -0.5) * (IDX_DIM ** -0.5))).T

    def idx_body(c, carry):
        ikc = ik_ref[pl.ds(pl.multiple_of(c * tq, tq), tq), :]
        acc = jnp.zeros((tq, tq), F32)
        for h in range(IDX_HEADS):
            d = _dot_nt(ikc, iq_ref[:, h * LANE:(h + 1) * LANE])
            acc = acc + jnp.maximum(d, 0.0) * w_t[h:h + 1, :]
        bits = lax.bitcast_convert_type(acc, jnp.int32)
        key = bits ^ ((bits >> 31) & jnp.int32(0x7FFFFFFF))
        for j in range(tq // kc):
            cj = c * (tq // kc) + j
            key_scr[cj] = jnp.where(cj * kc + kidx0 <= qidx, key[j * kc:(j + 1) * kc], INT_MIN)
        return carry

    lax.fori_loop(0, qi + 1, idx_body, 0)

    def count(pred):
        def body(c, cnt):
            hit = jnp.where(pred(key_scr[c], c), 1.0, 0.0)
            return cnt + jnp.sum(hit.reshape(kc // SUBLANES, SUBLANES, tq), axis=0)
        cnt = lax.fori_loop(0, nkc, body, jnp.zeros((SUBLANES, tq), F32))
        return jnp.sum(cnt, axis=0, keepdims=True)

    def bit_body(i, res):
        cand = res | jnp.left_shift(jnp.int32(1), 31 - i)
        thr_c = cand ^ INT_MIN
        total = count(lambda key, c: key >= thr_c)
        return jnp.where(total >= topk, cand, res)

    res = lax.fori_loop(0, 32, bit_body, jnp.zeros((1, tq), jnp.int32))
    thr = res ^ INT_MIN
    c_ge = count(lambda key, c: key >= thr)
    c_gt = count(lambda key, c: key > thr)
    need_eq = topk - c_gt
    tie = jnp.logical_and(c_ge > topk, thr != INT_MIN)
    j_scr[...] = jnp.full((1, tq), seq, jnp.int32)

    @pl.when(jnp.max(jnp.where(tie, 1.0, 0.0)) > 0.0)
    def _():
        def jbit_body(i, j):
            cand = j | jnp.left_shift(jnp.int32(1), (seq.bit_length() - 1) - i)
            below = count(lambda key, c: jnp.logical_and(key == thr, c * kc + kidx0 < cand))
            return jnp.where(below < need_eq, cand, j)

        j = lax.fori_loop(0, seq.bit_length(), jbit_body, jnp.zeros((1, tq), jnp.int32))
        j_scr[...] = jnp.where(tie, j, seq)

    jmax = j_scr[...]
    posq = posq_ref[...]

    def bias_body(c, carry):
        key = key_scr[c]
        kid = c * kc + kidx0
        keep = jnp.logical_or(key > thr, jnp.logical_and(key == thr, kid <= jmax))
        keep = jnp.logical_and(keep, kid <= qidx)
        dist = posq - posk_ref[pl.ds(pl.multiple_of(c * kc, kc), kc), :]
        bias_scr[c] = jnp.where(keep, dist, MASK_BIAS)
        return carry

    lax.fori_loop(0, nkc, bias_body, 0)

    m_scr[...] = jnp.full(m_scr.shape, NEG_BIG, F32)
    l_scr[...] = jnp.zeros(l_scr.shape, F32)
    acc_scr[...] = jnp.zeros(acc_scr.shape, F32)

    def attn_body(c, carry):
        start = pl.multiple_of(c * kc, kc)
        kch = kn_scr[pl.ds(start, kc), :]
        vch = va_ref[pl.ds(start, kc), :]
        bias = bias_scr[c]
        qk_next = _dot_nt(kch, qn_scr[0])
        for h in range(A_HEADS):
            qk = qk_next
            if h + 1 < A_HEADS:
                qk_next = _dot_nt(kch, qn_scr[h + 1])
            s = qk - ALIBI_SLOPES_LOG2[h] * bias
            m = m_scr[h]
            m_new = jnp.maximum(m, jnp.max(s, axis=0, keepdims=True))
            alpha = jnp.exp2(m - m_new)
            p = jnp.exp2(s - m_new)
            l_scr[h] = alpha * l_scr[h] + jnp.sum(p, axis=0, keepdims=True)
            acc_scr[h] = alpha * acc_scr[h] + _dot_tn(vch, p.astype(BF16))
            m_scr[h] = m_new
        return carry

    lax.fori_loop(0, nkc, attn_body, 0)

    for h in range(A_HEADS):
        g = ga_ref[:, h * LANE:(h + 1) * LANE].astype(F32)
        o = (acc_scr[h] / l_scr[h]).T
        out_ref[:, h * LANE:(h + 1) * LANE] = (o * _silu(g)).astype(BF16)


def _dsa(z, pos_f, qg, kg, batch, seq):
    t = batch * seq
    tq, kc = DSA_TQ, DSA_KC
    nq = seq // tq
    nc = seq // kc
    topk = min(IDX_TOPK_MAX, seq // 4)
    posq = pos_f.reshape(batch, nq, 1, tq)
    posk = pos_f.reshape(t, 1)
    wide = lambda blk: pl.BlockSpec((tq, A_WIDTH), lambda b, i: (b * nq + i, blk))
    keys = lambda blk: pl.BlockSpec((seq, LANE), lambda b, i: (b, blk))
    kern = functools.partial(_dsa_kernel, topk=topk, seq=seq)
    return pl.pallas_call(
        kern,
        out_shape=jax.ShapeDtypeStruct((t, A_WIDTH), BF16),
        grid=(batch, nq),
        in_specs=[wide(4), wide(5), wide(6),
                  pl.BlockSpec((tq, LANE), lambda b, i: (b * nq + i, 115)),
                  pl.BlockSpec((None, None, 1, tq), lambda b, i: (b, i, 0, 0)),
                  keys(112), keys(113), keys(114),
                  pl.BlockSpec((seq, 1), lambda b, i: (b, 0)),
                  pl.BlockSpec((1, LANE), lambda b, i: (0, 0)),
                  pl.BlockSpec((1, LANE), lambda b, i: (0, 0))],
        out_specs=pl.BlockSpec((tq, A_WIDTH), lambda b, i: (b * nq + i, 0)),
        scratch_shapes=[pltpu.VMEM((seq, LANE), BF16),
                        pltpu.VMEM((A_HEADS, tq, LANE), BF16),
                        pltpu.VMEM((nc, kc, tq), jnp.int32),
                        pltpu.VMEM((nc, kc, tq), F32),
                        pltpu.VMEM((A_HEADS, 1, tq), F32),
                        pltpu.VMEM((A_HEADS, 1, tq), F32),
                        pltpu.VMEM((A_HEADS, A_HEAD_DIM, tq), F32),
                        pltpu.VMEM((1, tq), jnp.int32)],
        compiler_params=_params(("arbitrary", "arbitrary"), VMEM_LIMIT),
        name="dsa",
    )(z, z, z, z, posq, z, z, z, posk, qg, kg)


RET_HG = 4


def _retention_kernel(cdec_ref, q_ref, k_ref, v_ref, g_ref, inner_ref, qdec_ref, kdec_ref, gn_ref,
                      out_ref, state_scr, *, seq):
    c = RET_CHUNK
    hg = pl.program_id(1)
    state_scr[...] = jnp.zeros_like(state_scr)

    def body(i, carry):
        rows = pl.ds(pl.multiple_of(i * c, c), c)
        cols = [slice(h * B_V_DIM, (h + 1) * B_V_DIM) for h in range(RET_HG)]
        q = [q_ref[rows, cols[h]] for h in range(RET_HG)]
        k = [k_ref[rows, cols[h]] for h in range(RET_HG)]
        v = [v_ref[rows, cols[h]] for h in range(RET_HG)]
        qk = [_dot_nt(q[h], k[h]) for h in range(RET_HG)]
        cross = [_dot(q[h], state_scr[h].astype(BF16)) for h in range(RET_HG)]
        kv = [_dot_tn((k[h].astype(F32) * kdec_ref[h]).astype(BF16), v[h]) for h in range(RET_HG)]
        for h in range(RET_HG):
            o = _dot((qk[h] * inner_ref[h]).astype(BF16), v[h]) + cross[h] * qdec_ref[h]
            state_scr[h] = state_scr[h] * cdec_ref[hg * RET_HG + h] + kv[h]
            oc = o - jnp.mean(o, axis=-1, keepdims=True)
            y = oc * lax.rsqrt(jnp.mean(oc * oc, axis=-1, keepdims=True) + NORM_EPS) * gn_ref[h]
            out_ref[rows, cols[h]] = (y * _silu(g_ref[rows, cols[h]].astype(F32))).astype(BF16)
        return carry

    lax.fori_loop(0, seq // c, body, 0)


def _retention(z, gn, batch, seq):
    t = batch * seq
    c = RET_CHUNK
    hg = RET_HG
    hs = np.arange(B_HEADS, dtype=np.float64)
    log_gamma = np.log1p(-(2.0 ** (-5.0 - hs)))
    n = np.arange(c, dtype=np.float64)
    diff = n[:, None] - n[None, :]
    scale = B_QK_DIM ** -0.5
    inner = np.where(diff[None] >= 0, np.exp(log_gamma[:, None, None] * np.maximum(diff, 0.0)[None]), 0.0) * scale
    qdec = np.exp(log_gamma[:, None] * (n[None, :] + 1.0))[:, :, None]
    kdec = np.exp(log_gamma[:, None] * (c - 1.0 - n[None, :]))[:, :, None] * scale
    cdec = np.exp(log_gamma * c)
    nb = B_HEADS // hg
    blk = lambda off: pl.BlockSpec((seq, hg * B_V_DIM), lambda b, h: (b, off + h))
    tab = lambda w: pl.BlockSpec((hg, c, w), lambda b, h: (h, 0, 0))
    return pl.pallas_call(
        functools.partial(_retention_kernel, seq=seq),
        out_shape=jax.ShapeDtypeStruct((t, B_WIDTH), BF16),
        grid=(batch, nb),
        in_specs=[pl.BlockSpec(memory_space=pltpu.SMEM),
                  blk(0), blk(nb), blk(2 * nb), blk(3 * nb),
                  tab(c), tab(1), tab(1),
                  pl.BlockSpec((hg, 1, B_V_DIM), lambda b, h: (h, 0, 0))],
        out_specs=pl.BlockSpec((seq, hg * B_V_DIM), lambda b, h: (b, h)),
        scratch_shapes=[pltpu.VMEM((hg, B_QK_DIM, B_V_DIM), F32)],
        compiler_params=_params(("arbitrary", "arbitrary"), VMEM_LIMIT),
        name="retention",
    )(jnp.asarray(cdec, F32), z, z, z, z, jnp.asarray(inner, F32), jnp.asarray(qdec, F32),
      jnp.asarray(kdec, F32), gn.reshape(B_HEADS, 1, B_V_DIM))


GMLP_TM = 256


def _gmlp_kernel(u_ref, v_ref, g_ref, vg_ref, ws_ref, bst_ref, out_ref):
    c = C_CHUNK
    tril = lax.broadcasted_iota(jnp.int32, (c, c), 0) >= lax.broadcasted_iota(jnp.int32, (c, c), 1)
    v = jax.nn.gelu(v_ref[...].astype(F32))
    vc = v - jnp.mean(v, axis=-1, keepdims=True)
    vn = (vc * lax.rsqrt(jnp.mean(vc * vc, axis=-1, keepdims=True) + NORM_EPS) * vg_ref[...]).astype(BF16)
    bst = bst_ref[...]
    for g in range(C_GROUPS):
        w = jnp.where(tril, ws_ref[g], 0.0).astype(BF16)
        cols = slice(g * C_GROUP_DIM, (g + 1) * C_GROUP_DIM)
        for j in range(GMLP_TM // c):
            rows = slice(j * c, (j + 1) * c)
            mixed = _dot(w, vn[rows, cols]) + bst[:, g:g + 1]
            u = jax.nn.gelu(u_ref[rows, cols].astype(F32))
            out_ref[rows, cols] = (u * mixed * _silu(g_ref[rows, cols].astype(F32))).astype(BF16)


def _gmlp(z, v_gain, w_s, b_s):
    t = z.shape[0]
    tm = GMLP_TM
    bst = jnp.pad(b_s.T, ((0, 0), (0, LANE - C_GROUPS)))
    blk = lambda j: pl.BlockSpec((tm, C_WIDTH), lambda i: (i, j))
    return pl.pallas_call(
        _gmlp_kernel,
        out_shape=jax.ShapeDtypeStruct((t, C_WIDTH), BF16),
        grid=(t // tm,),
        in_specs=[blk(0), blk(1), blk(2),
                  pl.BlockSpec((1, C_WIDTH), lambda i: (0, 0)),
                  pl.BlockSpec((C_GROUPS, C_CHUNK, C_CHUNK), lambda i: (0, 0, 0)),
                  pl.BlockSpec((C_CHUNK, LANE), lambda i: (0, 0))],
        out_specs=pl.BlockSpec((tm, C_WIDTH), lambda i: (i, 0)),
        compiler_params=_params(("arbitrary",), VMEM_LIMIT),
        name="gmlp",
    )(z, z, z, v_gain.reshape(1, C_WIDTH), w_s, bst)


MLA_TM = 256
MLA_TQ = 256
MLA_KC = 512
ROPE_HALF = D_ROPE // 2


def _rope_layout(v64):
    z = jnp.zeros(v64.shape[:-1] + (ROPE_HALF,), v64.dtype)
    return jnp.concatenate([v64[..., :ROPE_HALF], z, v64[..., ROPE_HALF:], z], axis=-1)


def _mla_prep_kernel(q_ref, kn_ref, kr_ref, pos_ref, tab_ref, qg_ref, kg_ref, qo_ref, ko_ref):
    tab = tab_ref[...]
    ang = pos_ref[...] * tab[0:1, :]
    cos = jnp.cos(ang) * tab[1:2, :]
    sin = jnp.sin(ang) * tab[2:3, :]

    def rope(x):
        return x * cos + pltpu.roll(x, LANE // 2, 1) * sin

    qg = qg_ref[...]
    kg = kg_ref[...]
    scale = D_QK ** -0.5 * LOG2E
    kr = kr_ref[...].astype(F32)
    kr_ss = jnp.sum(kr * kr, axis=-1, keepdims=True)
    kr_rot = rope(kr * kg[:, LANE:])
    for h in range(D_HEADS):
        lo = slice(h * D_HEAD_PAD, h * D_HEAD_PAD + LANE)
        hi = slice(h * D_HEAD_PAD + LANE, (h + 1) * D_HEAD_PAD)
        q1 = q_ref[:, lo].astype(F32)
        q2 = q_ref[:, hi].astype(F32)
        ss = jnp.sum(q1 * q1, axis=-1, keepdims=True) + jnp.sum(q2 * q2, axis=-1, keepdims=True)
        r = lax.rsqrt(ss * (1.0 / D_QK) + NORM_EPS) * scale
        qo_ref[:, lo] = (q1 * r * qg[:, :LANE]).astype(BF16)
        qo_ref[:, hi] = (rope(q2 * qg[:, LANE:]) * r).astype(BF16)
        k1 = kn_ref[:, h * LANE:(h + 1) * LANE].astype(F32)
        ss = jnp.sum(k1 * k1, axis=-1, keepdims=True) + kr_ss
        r = lax.rsqrt(ss * (1.0 / D_QK) + NORM_EPS)
        ko_ref[:, lo] = (k1 * r * kg[:, :LANE]).astype(BF16)
        ko_ref[:, hi] = (kr_rot * r).astype(BF16)


def _mla_prep(q_raw, kv, kr, pos_f, qn_g, kn_g):
    t = q_raw.shape[0]
    tm = MLA_TM
    inv = ROPE_BASE ** (-np.arange(ROPE_HALF, dtype=np.float64) / ROPE_HALF)
    zero = np.zeros(ROPE_HALF)
    one = np.ones(ROPE_HALF)
    tab = np.zeros((8, LANE))
    tab[0] = np.concatenate([inv, zero, inv, zero])
    tab[1] = np.concatenate([one, zero, one, zero])
    tab[2] = np.concatenate([-one, zero, one, zero])
    lay = lambda g: jnp.concatenate([g[:D_NOPE], _rope_layout(g[D_NOPE:])]).reshape(1, D_HEAD_PAD)
    width = D_HEADS * D_HEAD_PAD
    return pl.pallas_call(
        _mla_prep_kernel,
        out_shape=(jax.ShapeDtypeStruct((t, width), BF16), jax.ShapeDtypeStruct((t, width), BF16)),
        grid=(t // tm,),
        in_specs=[pl.BlockSpec((tm, width), lambda i: (i, 0)),
                  pl.BlockSpec((tm, D_HEADS * D_NOPE), lambda i: (i, 0)),
                  pl.BlockSpec((tm, LANE), lambda i: (i, 0)),
                  pl.BlockSpec((tm, 1), lambda i: (i, 0)),
                  pl.BlockSpec((8, LANE), lambda i: (0, 0)),
                  pl.BlockSpec((1, D_HEAD_PAD), lambda i: (0, 0)),
                  pl.BlockSpec((1, D_HEAD_PAD), lambda i: (0, 0))],
        out_specs=(pl.BlockSpec((tm, width), lambda i: (i, 0)),
                   pl.BlockSpec((tm, width), lambda i: (i, 0))),
        compiler_params=_params(("arbitrary",), VMEM_LIMIT),
        name="mla_prep",
    )(q_raw, kv, kr, pos_f.reshape(t, 1), jnp.asarray(tab, F32), lay(qn_g), lay(kn_g))


MLA_HG = 4


def _mla_attn_kernel(q_ref, k_ref, v_ref, g_ref, out_ref, m_scr, l_scr, acc_scr):
    tq, kc = MLA_TQ, MLA_KC
    qi = pl.program_id(2)
    m_scr[...] = jnp.full(m_scr.shape, NEG_BIG, F32)
    l_scr[...] = jnp.zeros(l_scr.shape, F32)
    acc_scr[...] = jnp.zeros(acc_scr.shape, F32)

    def step(c, diag):
        start = pl.multiple_of(c * kc, kc)
        if diag:
            keep = (c * kc + lax.broadcasted_iota(jnp.int32, (kc, tq), 0)
                    <= qi * tq + lax.broadcasted_iota(jnp.int32, (kc, tq), 1))
        def logits(h):
            return _dot_nt(k_ref[pl.ds(start, kc), h * D_HEAD_PAD:(h + 1) * D_HEAD_PAD],
                           q_ref[:, h * D_HEAD_PAD:(h + 1) * D_HEAD_PAD])

        s_next = logits(0)
        for h in range(MLA_HG):
            s = s_next
            if h + 1 < MLA_HG:
                s_next = logits(h + 1)
            if diag:
                s = jnp.where(keep, s, NEG_BIG)
            m = m_scr[h]
            m_new = jnp.maximum(m, jnp.max(s, axis=0, keepdims=True))
            alpha = jnp.exp2(m - m_new)
            p = jnp.exp2(s - m_new)
            l_scr[h] = alpha * l_scr[h] + jnp.sum(p, axis=0, keepdims=True)
            acc_scr[h] = alpha * acc_scr[h] + _dot_tn(v_ref[pl.ds(start, kc), h * D_V:(h + 1) * D_V], p.astype(BF16))
            m_scr[h] = m_new

    def body(c, carry):
        step(c, False)
        return carry

    n_full = (qi * tq) // kc
    lax.fori_loop(0, n_full, body, 0)
    step(n_full, True)
    for h in range(MLA_HG):
        g = g_ref[:, h * D_V:(h + 1) * D_V].astype(F32)
        o = (acc_scr[h] / l_scr[h]).T
        out_ref[:, h * D_V:(h + 1) * D_V] = (o * _silu(g)).astype(BF16)


def _mla_attn(q, k, kv, z, batch, seq):
    assert MLA_KC % MLA_TQ == 0
    t = batch * seq
    tq = MLA_TQ
    nq = seq // tq
    hg = MLA_HG
    v0 = D_HEADS * D_NOPE // (hg * D_V)
    gd0 = (ODD_MAIN - D_WIDTH) // (hg * D_V)
    return pl.pallas_call(
        _mla_attn_kernel,
        out_shape=jax.ShapeDtypeStruct((t, D_WIDTH), BF16),
        grid=(batch, D_HEADS // hg, nq),
        in_specs=[pl.BlockSpec((tq, hg * D_HEAD_PAD), lambda b, h, i: (b * nq + i, h)),
                  pl.BlockSpec((seq, hg * D_HEAD_PAD), lambda b, h, i: (b, h)),
                  pl.BlockSpec((seq, hg * D_V), lambda b, h, i: (b, v0 + h)),
                  pl.BlockSpec((tq, hg * D_V), lambda b, h, i: (b * nq + i, gd0 + h))],
        out_specs=pl.BlockSpec((tq, hg * D_V), lambda b, h, i: (b * nq + i, h)),
        scratch_shapes=[pltpu.VMEM((hg, 1, tq), F32),
                        pltpu.VMEM((hg, 1, tq), F32),
                        pltpu.VMEM((hg, D_V, tq), F32)],
        compiler_params=_params(("arbitrary", "arbitrary", "arbitrary"), VMEM_LIMIT),
        name="mla_attn",
    )(q, k, kv, z)


def _even_w_in(w, g):
    w = w * g[:, None]
    parts = dict(zip(EVEN_NAMES, jnp.split(w, [int(c) for c in np.cumsum(EVEN_SPLITS)[:-1]], axis=1)))
    cols = [parts[n] for n in EVEN_ORDER] + [jnp.zeros((w.shape[0], EVEN_IW_PAD), w.dtype)]
    return jnp.concatenate(cols, axis=1).astype(BF16)


def _odd_w_in(w, g):
    w = w * g[:, None]
    cu, cv, gc, cq, ckv, krope, gd = jnp.split(w, [int(c) for c in np.cumsum(ODD_SPLITS)[:-1]], axis=1)
    main = jnp.concatenate([cu, cv, gc, cq, ckv, gd], axis=1).astype(BF16)
    return main, _rope_layout(krope).astype(BF16)


def _w_uq_layout(w, g):
    w = (w * g[:, None]).reshape(D_Q_LORA, D_HEADS, D_QK)
    w = jnp.concatenate([w[..., :D_NOPE], _rope_layout(w[..., D_NOPE:])], axis=-1)
    return w.reshape(D_Q_LORA, D_HEADS * D_HEAD_PAD).astype(BF16)


def _w_ukv_layout(w, g):
    w = (w * g[:, None]).reshape(D_KV_LORA, D_HEADS, D_NOPE + D_V)
    return jnp.concatenate([w[..., :D_NOPE].reshape(D_KV_LORA, -1), w[..., D_NOPE:].reshape(D_KV_LORA, -1)],
                           axis=1).astype(BF16)


def _even_mixer(xb, pos_f, g_in, w_in, a_qn, a_kn, b_gn, batch, seq):
    z = _norm_proj(xb, _even_w_in(w_in, g_in), 1024, 512, name="even_in")
    a = _dsa(z, pos_f, a_qn.reshape(1, LANE), a_kn.reshape(1, LANE), batch, seq)
    b = _retention(z, b_gn, batch, seq)
    return a, b


def _odd_mixer(xb, pos_f, g_in, w_in, c_vn, c_ws, c_bs, d_qlg, d_kvlg, d_wuq, d_wukv, d_qn, d_kn, batch, seq):
    w_main, w_kr = _odd_w_in(w_in, g_in)
    z = _norm_proj(xb, w_main, 1024, 1024, name="odd_in")
    kr = _norm_proj(xb, w_kr, 1024, LANE, name="odd_in_rope")
    c = _gmlp(z, c_vn, c_ws, c_bs)
    cq_block = 3 * C_WIDTH // D_Q_LORA
    ckv_block = (3 * C_WIDTH + D_Q_LORA) // D_KV_LORA
    q_raw = _norm_proj(z, _w_uq_layout(d_wuq, d_qlg), 1024, 1024, col_block=cq_block, name="mla_uq")
    kv = _norm_proj(z, _w_ukv_layout(d_wukv, d_kvlg), 1024, 1024, col_block=ckv_block, name="mla_ukv")
    q, k = _mla_prep(q_raw, kv, kr, pos_f, d_qn, d_kn)
    d = _mla_attn(q, k, kv, z, batch, seq)
    return c, d


def kernel(x, p, positions, norm_in, even_w_in, even_a_q_norm, even_a_k_norm, even_b_group_norm, even_w_out, odd_w_in, odd_c_v_norm, odd_c_w_s, odd_c_b_s, odd_d_q_lora_norm, odd_d_kv_lora_norm, odd_d_w_uq, odd_d_w_ukv, odd_d_q_norm, odd_d_k_norm, odd_w_out, ple_w_in, ple_w_gate):
    batch, seq, d = x.shape
    t = batch * seq
    depth = norm_in.shape[0]
    xs = x.reshape(t, d)
    xb = xs.astype(BF16)
    pos_f = positions.astype(F32)
    p2 = p.reshape(depth, t, PLE_DIM)
    w_out = (even_w_out.astype(BF16), odd_w_out.astype(BF16))
    w_gate = ple_w_gate.astype(BF16)
    w_pin = ple_w_in.astype(BF16)
    for i in range(depth):
        j = i // 2
        if i % 2 == 0:
            m1, m2 = _even_mixer(xb, pos_f, norm_in[i], even_w_in[j], even_a_q_norm[j], even_a_k_norm[j],
                                 even_b_group_norm[j], batch, seq)
        else:
            m1, m2 = _odd_mixer(xb, pos_f, norm_in[i], odd_w_in[j], odd_c_v_norm[j], odd_c_w_s[j], odd_c_b_s[j],
                                odd_d_q_lora_norm[j], odd_d_kv_lora_norm[j], odd_d_w_uq[j], odd_d_w_ukv[j],
                                odd_d_q_norm[j], odd_d_k_norm[j], batch, seq)
        xs, xb = _out_proj(m1, m2, w_out[i % 2], j, xs, 1024, 512)
        xs, xb = _ple(xb, w_gate, p2, w_pin, i, xs, 1024, 512)
    return xs.reshape(batch, seq, d)
```

```python
import functools
import math

import numpy as np
import jax
import jax.numpy as jnp
from jax import lax
from jax.experimental import pallas as pl
from jax.experimental.pallas import tpu as pltpu

F32 = jnp.float32
BF16 = jnp.bfloat16

D_MODEL = 4096
PLE_DIM = 256
A_HEADS = 16
A_HEAD_DIM = 128
IDX_HEADS = 16
IDX_DIM = 128
IDX_TOPK_MAX = 256
B_HEADS = 8
B_QK_DIM = 256
B_V_DIM = 256
RET_CHUNK = 128
C_GROUPS = 16
C_GROUP_DIM = 128
C_CHUNK = 128
D_HEADS = 16
D_NOPE = 128
D_ROPE = 64
D_V = 128
D_Q_LORA = 1536
D_KV_LORA = 512
ROPE_BASE = 10000.0
A_WIDTH = A_HEADS * A_HEAD_DIM
B_WIDTH = B_HEADS * B_V_DIM
C_WIDTH = C_GROUPS * C_GROUP_DIM
D_WIDTH = D_HEADS * D_V
D_QK = D_NOPE + D_ROPE
EVEN_SPLITS = (A_WIDTH, A_HEAD_DIM, A_HEAD_DIM, IDX_HEADS * IDX_DIM, IDX_DIM, IDX_HEADS, A_WIDTH,
               B_HEADS * B_QK_DIM, B_HEADS * B_QK_DIM, B_WIDTH, B_WIDTH)
ODD_SPLITS = (C_WIDTH, C_WIDTH, C_WIDTH, D_Q_LORA, D_KV_LORA, D_ROPE, D_WIDTH)

LANE = 128
D_HEAD_PAD = 256
VMEM_LIMIT = 56 * 1024 * 1024
NORM_EPS = 1e-6
MASK_BIAS = 1e30
NEG_BIG = -1e30
INT_MIN = np.int32(-2 ** 31)

EVEN_ORDER = ("qb", "kb", "vb", "gb", "qa", "iq", "ga", "ka", "va", "ik", "iw")
EVEN_NAMES = ("qa", "ka", "va", "iq", "ik", "iw", "ga", "qb", "kb", "vb", "gb")
EVEN_IW_PAD = LANE - IDX_HEADS
EVEN_OUT = sum(EVEN_SPLITS) + EVEN_IW_PAD
ODD_MAIN = sum(ODD_SPLITS) - D_ROPE


def _params(sem, vmem=None):
    return pltpu.CompilerParams(dimension_semantics=sem, vmem_limit_bytes=vmem)


def _dot(a, b):
    return jnp.dot(a, b, preferred_element_type=F32)


def _dot_nt(a, b):
    return lax.dot_general(a, b, (((1,), (1,)), ((), ())), preferred_element_type=F32)


def _dot_tn(a, b):
    return lax.dot_general(a, b, (((0,), (0,)), ((), ())), preferred_element_type=F32)


def _silu(x):
    return x * (1.0 / (1.0 + jnp.exp(-x)))


def _norm_proj_kernel(x_ref, w_ref, o_ref, r_scr, xb_scr):
    @pl.when(pl.program_id(1) == 0)
    def _():
        x = x_ref[...].astype(F32)
        r_scr[...] = lax.rsqrt(jnp.mean(x * x, axis=-1, keepdims=True) + NORM_EPS)
        xb_scr[...] = x_ref[...].astype(BF16)

    o_ref[...] = (_dot(xb_scr[...], w_ref[...]) * r_scr[...]).astype(o_ref.dtype)


def _norm_proj(x, w, tm, tn, col_block=0, name="norm_proj"):
    m = x.shape[0]
    k, n = w.shape
    return pl.pallas_call(
        _norm_proj_kernel,
        out_shape=jax.ShapeDtypeStruct((m, n), BF16),
        grid=(m // tm, n // tn),
        in_specs=[pl.BlockSpec((tm, k), lambda i, j: (i, col_block)),
                  pl.BlockSpec((k, tn), lambda i, j: (0, j))],
        out_specs=pl.BlockSpec((tm, tn), lambda i, j: (i, j)),
        scratch_shapes=[pltpu.VMEM((tm, 1), F32), pltpu.VMEM((tm, k), BF16)],
        compiler_params=_params(("arbitrary", "arbitrary"), VMEM_LIMIT),
        name=name,
    )(x, w)


def _out_proj_kernel(a_ref, b_ref, wa_ref, wb_ref, x_ref, o_ref, ob_ref):
    y = x_ref[...] + _dot(a_ref[...], wa_ref[...]) + _dot(b_ref[...], wb_ref[...])
    o_ref[...] = y
    ob_ref[...] = y.astype(BF16)


def _out_proj(a, b, w, layer, x, tm, tn):
    m, kh = a.shape
    n = w.shape[2]
    return pl.pallas_call(
        _out_proj_kernel,
        out_shape=(jax.ShapeDtypeStruct((m, n), F32), jax.ShapeDtypeStruct((m, n), BF16)),
        grid=(m // tm, n // tn),
        in_specs=[pl.BlockSpec((tm, kh), lambda i, j: (i, 0)),
                  pl.BlockSpec((tm, kh), lambda i, j: (i, 0)),
                  pl.BlockSpec((None, kh, tn), lambda i, j: (layer, 0, j)),
                  pl.BlockSpec((None, kh, tn), lambda i, j: (layer, 1, j)),
                  pl.BlockSpec((tm, tn), lambda i, j: (i, j))],
        out_specs=(pl.BlockSpec((tm, tn), lambda i, j: (i, j)),
                   pl.BlockSpec((tm, tn), lambda i, j: (i, j))),
        compiler_params=_params(("arbitrary", "arbitrary"), VMEM_LIMIT),
        name="out_proj",
    )(a, b, w, w, x)


def _ple_kernel(xb_ref, wg_ref, p_ref, wp_ref, x_ref, o_ref, ob_ref):
    gate = _dot(xb_ref[...], wg_ref[...])
    emb = _dot(p_ref[...].astype(BF16), wp_ref[...])
    y = x_ref[...] + emb * (1.0 / (1.0 + jnp.exp(-gate)))
    o_ref[...] = y
    ob_ref[...] = y.astype(BF16)


def _ple(xb, wg, p, wp, layer, x, tm, tn):
    m, d = xb.shape
    n = wg.shape[2]
    pd = p.shape[2]
    return pl.pallas_call(
        _ple_kernel,
        out_shape=(jax.ShapeDtypeStruct((m, n), F32), jax.ShapeDtypeStruct((m, n), BF16)),
        grid=(m // tm, n // tn),
        in_specs=[pl.BlockSpec((tm, d), lambda i, j: (i, 0)),
                  pl.BlockSpec((None, d, tn), lambda i, j: (layer, 0, j)),
                  pl.BlockSpec((None, tm, pd), lambda i, j: (layer, i, 0)),
                  pl.BlockSpec((None, pd, tn), lambda i, j: (layer, 0, j)),
                  pl.BlockSpec((tm, tn), lambda i, j: (i, j))],
        out_specs=(pl.BlockSpec((tm, tn), lambda i, j: (i, j)),
                   pl.BlockSpec((tm, tn), lambda i, j: (i, j))),
        compiler_params=_params(("arbitrary", "arbitrary"), VMEM_LIMIT),
        name="ple",
    )(xb, wg, p, wp, x)


DSA_TQ = 256
DSA_KC = 128
SUBLANES = 8
LOG2E = math.log2(math.e)
ALIBI_SLOPES_LOG2 = tuple(float(2.0 ** (-8.0 * (h + 1) / A_HEADS)) * LOG2E for h in range(A_HEADS))


def _dsa_kernel(qa_ref, iq_ref, ga_ref, iw_ref, posq_ref, ka_ref, va_ref, ik_ref, posk_ref, qg_ref, kg_ref,
                out_ref, kn_scr, qn_scr, key_scr, bias_scr, m_scr, l_scr, acc_scr, j_scr, *, topk, seq):
    tq, kc = DSA_TQ, DSA_KC
    qi = pl.program_id(1)
    nkc = (qi + 1) * (tq // kc)

    @pl.when(qi == 0)
    def _():
        k = ka_ref[...].astype(F32)
        r = lax.rsqrt(jnp.mean(k * k, axis=-1, keepdims=True) + NORM_EPS)
        kn_scr[...] = (k * r * kg_ref[...]).astype(BF16)

    for h in range(A_HEADS):
        q = qa_ref[:, h * LANE:(h + 1) * LANE].astype(F32)
        r = lax.rsqrt(jnp.mean(q * q, axis=-1, keepdims=True) + NORM_EPS)
        qn_scr[h] = (q * r * (qg_ref[...] * (A_HEAD_DIM ** -0.5 * LOG2E))).astype(BF16)

    kidx0 = lax.broadcasted_iota(jnp.int32, (kc, tq), 0)
    qidx = qi * tq + lax.broadcasted_iota(jnp.int32, (kc, tq), 1)

    w_t = (iw_ref[...].astype(F32) * ((IDX_HEADS ** -0.5) * (IDX_DIM ** -0.5))).T

    def idx_body(c, carry):
        ikc = ik_ref[pl.ds(pl.multiple_of(c * tq, tq), tq), :]
        acc = jnp.zeros((tq, tq), F32)
        for h in range(IDX_HEADS):
            d = _dot_nt(ikc, iq_ref[:, h * LANE:(h + 1) * LANE])
            acc = acc + jnp.maximum(d, 0.0) * w_t[h:h + 1, :]
        bits = lax.bitcast_convert_type(acc, jnp.int32)
        key = bits ^ ((bits >> 31) & jnp.int32(0x7FFFFFFF))
        for j in range(tq // kc):
            cj = c * (tq // kc) + j
            key_scr[cj] = jnp.where(cj * kc + kidx0 <= qidx, key[j * kc:(j + 1) * kc], INT_MIN)
        return carry

    lax.fori_loop(0, qi + 1, idx_body, 0)

    def count(pred):
        def body(c, cnt):
            hit = jnp.where(pred(key_scr[c], c), 1.0, 0.0)
            return cnt + jnp.sum(hit.reshape(kc // SUBLANES, SUBLANES, tq), axis=0)
        cnt = lax.fori_loop(0, nkc, body, jnp.zeros((SUBLANES, tq), F32))
        return jnp.sum(cnt, axis=0, keepdims=True)

    def bit_body(i, res):
        cand = res | jnp.left_shift(jnp.int32(1), 31 - i)
        thr_c = cand ^ INT_MIN
        total = count(lambda key, c: key >= thr_c)
        return jnp.where(total >= topk, cand, res)

    res = lax.fori_loop(0, 32, bit_body, jnp.zeros((1, tq), jnp.int32))
    thr = res ^ INT_MIN
    c_ge = count(lambda key, c: key >= thr)
    c_gt = count(lambda key, c: key > thr)
    need_eq = topk - c_gt
    tie = jnp.logical_and(c_ge > topk, thr != INT_MIN)
    j_scr[...] = jnp.full((1, tq), seq, jnp.int32)

    @pl.when(jnp.max(jnp.where(tie, 1.0, 0.0)) > 0.0)
    def _():
        def jbit_body(i, j):
            cand = j | jnp.left_shift(jnp.int32(1), (seq.bit_length() - 1) - i)
            below = count(lambda key, c: jnp.logical_and(key == thr, c * kc + kidx0 < cand))
            return jnp.where(below < need_eq, cand, j)

        j = lax.fori_loop(0, seq.bit_length(), jbit_body, jnp.zeros((1, tq), jnp.int32))
        j_scr[...] = jnp.where(tie, j, seq)

    jmax = j_scr[...]
    posq = posq_ref[...]

    def bias_body(c, carry):
        key = key_scr[c]
        kid = c * kc + kidx0
        keep = jnp.logical_or(key > thr, jnp.logical_and(key == thr, kid <= jmax))
        keep = jnp.logical_and(keep, kid <= qidx)
        dist = posq - posk_ref[pl.ds(pl.multiple_of(c * kc, kc), kc), :]
        bias_scr[c] = jnp.where(keep, dist, MASK_BIAS)
        return carry

    lax.fori_loop(0, nkc, bias_body, 0)

    m_scr[...] = jnp.full(m_scr.shape, NEG_BIG, F32)
    l_scr[...] = jnp.zeros(l_scr.shape, F32)
    acc_scr[...] = jnp.zeros(acc_scr.shape, F32)

    def attn_body(c, carry):
        start = pl.multiple_of(c * kc, kc)
        kch = kn_scr[pl.ds(start, kc), :]
        vch = va_ref[pl.ds(start, kc), :]
        bias = bias_scr[c]
        qk_next = _dot_nt(kch, qn_scr[0])
        for h in range(A_HEADS):
            qk = qk_next
            if h + 1 < A_HEADS:
                qk_next = _dot_nt(kch, qn_scr[h + 1])
            s = qk - ALIBI_SLOPES_LOG2[h] * bias
            m = m_scr[h]
            m_new = jnp.maximum(m, jnp.max(s, axis=0, keepdims=True))
            alpha = jnp.exp2(m - m_new)
            p = jnp.exp2(s - m_new)
            l_scr[h] = alpha * l_scr[h] + jnp.sum(p, axis=0, keepdims=True)
            acc_scr[h] = alpha * acc_scr[h] + _dot_tn(vch, p.astype(BF16))
            m_scr[h] = m_new
        return carry

    lax.fori_loop(0, nkc, attn_body, 0)

    for h in range(A_HEADS):
        g = ga_ref[:, h * LANE:(h + 1) * LANE].astype(F32)
        o = (acc_scr[h] / l_scr[h]).T
        out_ref[:, h * LANE:(h + 1) * LANE] = (o * _silu(g)).astype(BF16)


def _dsa(z, pos_f, qg, kg, batch, seq):
    t = batch * seq
    tq, kc = DSA_TQ, DSA_KC
    nq = seq // tq
    nc = seq // kc
    topk = min(IDX_TOPK_MAX, seq // 4)
    posq = pos_f.reshape(batch, nq, 1, tq)
    posk = pos_f.reshape(t, 1)
    wide = lambda blk: pl.BlockSpec((tq, A_WIDTH), lambda b, i: (b * nq + i, blk))
    keys = lambda blk: pl.BlockSpec((seq, LANE), lambda b, i: (b, blk))
    kern = functools.partial(_dsa_kernel, topk=topk, seq=seq)
    return pl.pallas_call(
        kern,
        out_shape=jax.ShapeDtypeStruct((t, A_WIDTH), BF16),
        grid=(batch, nq),
        in_specs=[wide(4), wide(5), wide(6),
                  pl.BlockSpec((tq, LANE), lambda b, i: (b * nq + i, 115)),
                  pl.BlockSpec((None, None, 1, tq), lambda b, i: (b, i, 0, 0)),
                  keys(112), keys(113), keys(114),
                  pl.BlockSpec((seq, 1), lambda b, i: (b, 0)),
                  pl.BlockSpec((1, LANE), lambda b, i: (0, 0)),
                  pl.BlockSpec((1, LANE), lambda b, i: (0, 0))],
        out_specs=pl.BlockSpec((tq, A_WIDTH), lambda b, i: (b * nq + i, 0)),
        scratch_shapes=[pltpu.VMEM((seq, LANE), BF16),
                        pltpu.VMEM((A_HEADS, tq, LANE), BF16),
                        pltpu.VMEM((nc, kc, tq), jnp.int32),
                        pltpu.VMEM((nc, kc, tq), F32),
                        pltpu.VMEM((A_HEADS, 1, tq), F32),
                        pltpu.VMEM((A_HEADS, 1, tq), F32),
                        pltpu.VMEM((A_HEADS, A_HEAD_DIM, tq), F32),
                        pltpu.VMEM((1, tq), jnp.int32)],
        compiler_params=_params(("arbitrary", "arbitrary"), VMEM_LIMIT),
        name="dsa",
    )(z, z, z, z, posq, z, z, z, posk, qg, kg)


RET_HG = 4


def _retention_kernel(cdec_ref, q_ref, k_ref, v_ref, g_ref, inner_ref, qdec_ref, kdec_ref, gn_ref,
                      out_ref, state_scr, *, seq):
    c = RET_CHUNK
    hg = pl.program_id(1)
    state_scr[...] = jnp.zeros_like(state_scr)

    def body(i, carry):
        rows = pl.ds(pl.multiple_of(i * c, c), c)
        cols = [slice(h * B_V_DIM, (h + 1) * B_V_DIM) for h in range(RET_HG)]
        q = [q_ref[rows, cols[h]] for h in range(RET_HG)]
        k = [k_ref[rows, cols[h]] for h in range(RET_HG)]
        v = [v_ref[rows, cols[h]] for h in range(RET_HG)]
        qk = [_dot_nt(q[h], k[h]) for h in range(RET_HG)]
        cross = [_dot(q[h], state_scr[h].astype(BF16)) for h in range(RET_HG)]
        kv = [_dot_tn((k[h].astype(F32) * kdec_ref[h]).astype(BF16), v[h]) for h in range(RET_HG)]
        for h in range(RET_HG):
            o = _dot((qk[h] * inner_ref[h]).astype(BF16), v[h]) + cross[h] * qdec_ref[h]
            state_scr[h] = state_scr[h] * cdec_ref[hg * RET_HG + h] + kv[h]
            oc = o - jnp.mean(o, axis=-1, keepdims=True)
            y = oc * lax.rsqrt(jnp.mean(oc * oc, axis=-1, keepdims=True) + NORM_EPS) * gn_ref[h]
            out_ref[rows, cols[h]] = (y * _silu(g_ref[rows, cols[h]].astype(F32))).astype(BF16)
        return carry

    lax.fori_loop(0, seq // c, body, 0)


def _retention(z, gn, batch, seq):
    t = batch * seq
    c = RET_CHUNK
    hg = RET_HG
    hs = np.arange(B_HEADS, dtype=np.float64)
    log_gamma = np.log1p(-(2.0 ** (-5.0 - hs)))
    n = np.arange(c, dtype=np.float64)
    diff = n[:, None] - n[None, :]
    scale = B_QK_DIM ** -0.5
    inner = np.where(diff[None] >= 0, np.exp(log_gamma[:, None, None] * np.maximum(diff, 0.0)[None]), 0.0) * scale
    qdec = np.exp(log_gamma[:, None] * (n[None, :] + 1.0))[:, :, None]
    kdec = np.exp(log_gamma[:, None] * (c - 1.0 - n[None, :]))[:, :, None] * scale
    cdec = np.exp(log_gamma * c)
    nb = B_HEADS // hg
    blk = lambda off: pl.BlockSpec((seq, hg * B_V_DIM), lambda b, h: (b, off + h))
    tab = lambda w: pl.BlockSpec((hg, c, w), lambda b, h: (h, 0, 0))
    return pl.pallas_call(
        functools.partial(_retention_kernel, seq=seq),
        out_shape=jax.ShapeDtypeStruct((t, B_WIDTH), BF16),
        grid=(batch, nb),
        in_specs=[pl.BlockSpec(memory_space=pltpu.SMEM),
                  blk(0), blk(nb), blk(2 * nb), blk(3 * nb),
                  tab(c), tab(1), tab(1),
                  pl.BlockSpec((hg, 1, B_V_DIM), lambda b, h: (h, 0, 0))],
        out_specs=pl.BlockSpec((seq, hg * B_V_DIM), lambda b, h: (b, h)),
        scratch_shapes=[pltpu.VMEM((hg, B_QK_DIM, B_V_DIM), F32)],
        compiler_params=_params(("arbitrary", "arbitrary"), VMEM_LIMIT),
        name="retention",
    )(jnp.asarray(cdec, F32), z, z, z, z, jnp.asarray(inner, F32), jnp.asarray(qdec, F32),
      jnp.asarray(kdec, F32), gn.reshape(B_HEADS, 1, B_V_DIM))


GMLP_TM = 256


def _gmlp_kernel(u_ref, v_ref, g_ref, vg_ref, ws_ref, bst_ref, out_ref):
    c = C_CHUNK
    tril = lax.broadcasted_iota(jnp.int32, (c, c), 0) >= lax.broadcasted_iota(jnp.int32, (c, c), 1)
    v = jax.nn.gelu(v_ref[...].astype(F32))
    vc = v - jnp.mean(v, axis=-1, keepdims=True)
    vn = (vc * lax.rsqrt(jnp.mean(vc * vc, axis=-1, keepdims=True) + NORM_EPS) * vg_ref[...]).astype(BF16)
    bst = bst_ref[...]
    for g in range(C_GROUPS):
        w = jnp.where(tril, ws_ref[g], 0.0).astype(BF16)
        cols = slice(g * C_GROUP_DIM, (g + 1) * C_GROUP_DIM)
        for j in range(GMLP_TM // c):
            rows = slice(j * c, (j + 1) * c)
            mixed = _dot(w, vn[rows, cols]) + bst[:, g:g + 1]
            u = jax.nn.gelu(u_ref[rows, cols].astype(F32))
            out_ref[rows, cols] = (u * mixed * _silu(g_ref[rows, cols].astype(F32))).astype(BF16)


def _gmlp(z, v_gain, w_s, b_s):
    t = z.shape[0]
    tm = GMLP_TM
    bst = jnp.pad(b_s.T, ((0, 0), (0, LANE - C_GROUPS)))
    blk = lambda j: pl.BlockSpec((tm, C_WIDTH), lambda i: (i, j))
    return pl.pallas_call(
        _gmlp_kernel,
        out_shape=jax.ShapeDtypeStruct((t, C_WIDTH), BF16),
        grid=(t // tm,),
        in_specs=[blk(0), blk(1), blk(2),
                  pl.BlockSpec((1, C_WIDTH), lambda i: (0, 0)),
                  pl.BlockSpec((C_GROUPS, C_CHUNK, C_CHUNK), lambda i: (0, 0, 0)),
                  pl.BlockSpec((C_CHUNK, LANE), lambda i: (0, 0))],
        out_specs=pl.BlockSpec((tm, C_WIDTH), lambda i: (i, 0)),
        compiler_params=_params(("arbitrary",), VMEM_LIMIT),
        name="gmlp",
    )(z, z, z, v_gain.reshape(1, C_WIDTH), w_s, bst)


MLA_TM = 256
MLA_TQ = 256
MLA_KC = 512
ROPE_HALF = D_ROPE // 2


def _rope_layout(v64):
    return jnp.concatenate([v64, jnp.zeros(v64.shape[:-1] + (LANE - D_ROPE,), v64.dtype)], axis=-1)


def _mla_prep_kernel(q_ref, kn_ref, kr_ref, pos_ref, tab_ref, qg_ref, kg_ref, qo_ref, ko_ref):
    tab = tab_ref[...]
    ang = pos_ref[...] * tab[0:1, :]
    cos = jnp.cos(ang) * tab[1:2, :]
    sin = jnp.sin(ang)
    sin_lo = sin * tab[2:3, :]
    sin_hi = sin * tab[3:4, :]

    def rope(x):
        return (x * cos + pltpu.roll(x, LANE - ROPE_HALF, 1) * sin_lo + pltpu.roll(x, ROPE_HALF, 1) * sin_hi)

    qg = qg_ref[...]
    kg = kg_ref[...]
    scale = D_QK ** -0.5 * LOG2E
    kr = kr_ref[...].astype(F32)
    kr_ss = jnp.sum(kr * kr, axis=-1, keepdims=True)
    kr_rot = rope(kr * kg[:, LANE:])
    for h in range(D_HEADS):
        lo = slice(h * D_HEAD_PAD, h * D_HEAD_PAD + LANE)
        hi = slice(h * D_HEAD_PAD + LANE, (h + 1) * D_HEAD_PAD)
        q1 = q_ref[:, lo].astype(F32)
        q2 = q_ref[:, hi].astype(F32)
        ss = jnp.sum(q1 * q1, axis=-1, keepdims=True) + jnp.sum(q2 * q2, axis=-1, keepdims=True)
        r = lax.rsqrt(ss * (1.0 / D_QK) + NORM_EPS) * scale
        qo_ref[:, lo] = (q1 * r * qg[:, :LANE]).astype(BF16)
        qo_ref[:, hi] = (rope(q2 * qg[:, LANE:]) * r).astype(BF16)
        k1 = kn_ref[:, h * LANE:(h + 1) * LANE].astype(F32)
        ss = jnp.sum(k1 * k1, axis=-1, keepdims=True) + kr_ss
        r = lax.rsqrt(ss * (1.0 / D_QK) + NORM_EPS)
        ko_ref[:, lo] = (k1 * r * kg[:, :LANE]).astype(BF16)
        ko_ref[:, hi] = (kr_rot * r).astype(BF16)


def _mla_prep(q_raw, kv, kr, pos_f, qn_g, kn_g):
    t = q_raw.shape[0]
    tm = MLA_TM
    inv = ROPE_BASE ** (-np.arange(ROPE_HALF, dtype=np.float64) / ROPE_HALF)
    zero = np.zeros(ROPE_HALF)
    one = np.ones(ROPE_HALF)
    tab = np.zeros((8, LANE))
    tab[0] = np.concatenate([inv, inv, zero, zero])
    tab[1] = np.concatenate([one, one, zero, zero])
    tab[2] = np.concatenate([-one, zero, zero, zero])
    tab[3] = np.concatenate([zero, one, zero, zero])
    lay = lambda g: jnp.concatenate([g[:D_NOPE], _rope_layout(g[D_NOPE:])]).reshape(1, D_HEAD_PAD)
    width = D_HEADS * D_HEAD_PAD
    return pl.pallas_call(
        _mla_prep_kernel,
        out_shape=(jax.ShapeDtypeStruct((t, width), BF16), jax.ShapeDtypeStruct((t, width), BF16)),
        grid=(t // tm,),
        in_specs=[pl.BlockSpec((tm, width), lambda i: (i, 0)),
                  pl.BlockSpec((tm, D_HEADS * D_NOPE), lambda i: (i, 0)),
                  pl.BlockSpec((tm, LANE), lambda i: (i, 0)),
                  pl.BlockSpec((tm, 1), lambda i: (i, 0)),
                  pl.BlockSpec((8, LANE), lambda i: (0, 0)),
                  pl.BlockSpec((1, D_HEAD_PAD), lambda i: (0, 0)),
                  pl.BlockSpec((1, D_HEAD_PAD), lambda i: (0, 0))],
        out_specs=(pl.BlockSpec((tm, width), lambda i: (i, 0)),
                   pl.BlockSpec((tm, width), lambda i: (i, 0))),
        compiler_params=_params(("arbitrary",), VMEM_LIMIT),
        name="mla_prep",
    )(q_raw, kv, kr, pos_f.reshape(t, 1), jnp.asarray(tab, F32), lay(qn_g), lay(kn_g))


MLA_HG = 4


def _mla_attn_kernel(q_ref, k_ref, v_ref, g_ref, out_ref, m_scr, l_scr, acc_scr):
    tq, kc = MLA_TQ, MLA_KC
    qi = pl.program_id(2)
    m_scr[...] = jnp.full(m_scr.shape, NEG_BIG, F32)
    l_scr[...] = jnp.zeros(l_scr.shape, F32)
    acc_scr[...] = jnp.zeros(acc_scr.shape, F32)

    def step(c, diag):
        start = pl.multiple_of(c * kc, kc)
        if diag:
            keep = (c * kc + lax.broadcasted_iota(jnp.int32, (kc, tq), 0)
                    <= qi * tq + lax.broadcasted_iota(jnp.int32, (kc, tq), 1))
        def logits(h):
            return _dot_nt(k_ref[pl.ds(start, kc), h * D_HEAD_PAD:(h + 1) * D_HEAD_PAD],
                           q_ref[:, h * D_HEAD_PAD:(h + 1) * D_HEAD_PAD])

        s_next = logits(0)
        for h in range(MLA_HG):
            s = s_next
            if h + 1 < MLA_HG:
                s_next = logits(h + 1)
            if diag:
                s = jnp.where(keep, s, NEG_BIG)
            m = m_scr[h]
            m_new = jnp.maximum(m, jnp.max(s, axis=0, keepdims=True))
            alpha = jnp.exp2(m - m_new)
            p = jnp.exp2(s - m_new)
            l_scr[h] = alpha * l_scr[h] + jnp.sum(p, axis=0, keepdims=True)
            acc_scr[h] = alpha * acc_scr[h] + _dot_tn(v_ref[pl.ds(start, kc), h * D_V:(h + 1) * D_V], p.astype(BF16))
            m_scr[h] = m_new

    def body(c, carry):
        step(c, False)
        return carry

    n_full = (qi * tq) // kc
    lax.fori_loop(0, n_full, body, 0)
    step(n_full, True)
    for h in range(MLA_HG):
        g = g_ref[:, h * D_V:(h + 1) * D_V].astype(F32)
        o = (acc_scr[h] / l_scr[h]).T
        out_ref[:, h * D_V:(h + 1) * D_V] = (o * _silu(g)).astype(BF16)


def _mla_attn(q, k, kv, z, batch, seq):
    assert MLA_KC % MLA_TQ == 0
    t = batch * seq
    tq = MLA_TQ
    nq = seq // tq
    hg = MLA_HG
    v0 = D_HEADS * D_NOPE // (hg * D_V)
    gd0 = (ODD_MAIN - D_WIDTH) // (hg * D_V)
    return pl.pallas_call(
        _mla_attn_kernel,
        out_shape=jax.ShapeDtypeStruct((t, D_WIDTH), BF16),
        grid=(batch, D_HEADS // hg, nq),
        in_specs=[pl.BlockSpec((tq, hg * D_HEAD_PAD), lambda b, h, i: (b * nq + i, h)),
                  pl.BlockSpec((seq, hg * D_HEAD_PAD), lambda b, h, i: (b, h)),
                  pl.BlockSpec((seq, hg * D_V), lambda b, h, i: (b, v0 + h)),
                  pl.BlockSpec((tq, hg * D_V), lambda b, h, i: (b * nq + i, gd0 + h))],
        out_specs=pl.BlockSpec((tq, hg * D_V), lambda b, h, i: (b * nq + i, h)),
        scratch_shapes=[pltpu.VMEM((hg, 1, tq), F32),
                        pltpu.VMEM((hg, 1, tq), F32),
                        pltpu.VMEM((hg, D_V, tq), F32)],
        compiler_params=_params(("arbitrary", "arbitrary", "arbitrary"), VMEM_LIMIT),
        name="mla_attn",
    )(q, k, kv, z)


WPREP_TK = 128
WPREP_MAX_PIECE = 2048


def _wprep_kernel(w_ref, g_ref, *o_refs, plans):
    g = g_ref[...]
    for o_ref, plan in zip(o_refs, plans):
        for dst, pieces in plan:
            vals = [jnp.zeros((WPREP_TK, wd), F32) if src is None else w_ref[:, src:src + wd] for src, wd in pieces]
            v = vals[0] if len(vals) == 1 else jnp.concatenate(vals, axis=1)
            o_ref[:, dst:dst + v.shape[1]] = (v * g).astype(BF16)


def _wprep(w, layer, g, plans, widths, name):
    _, k, n = w.shape
    return pl.pallas_call(
        functools.partial(_wprep_kernel, plans=plans),
        out_shape=tuple(jax.ShapeDtypeStruct((k, wd), BF16) for wd in widths),
        grid=(k // WPREP_TK,),
        in_specs=[pl.BlockSpec((None, WPREP_TK, n), lambda i: (layer, i, 0)),
                  pl.BlockSpec((WPREP_TK, 1), lambda i: (i, 0))],
        out_specs=tuple(pl.BlockSpec((WPREP_TK, wd), lambda i: (i, 0)) for wd in widths),
        compiler_params=_params(("arbitrary",), VMEM_LIMIT),
        name=name,
    )(w, g.reshape(k, 1))


def _move(plan, dst, src, width):
    for off in range(0, width, WPREP_MAX_PIECE):
        plan.append((dst + off, [(src + off, min(WPREP_MAX_PIECE, width - off))]))


def _rope_pieces(src):
    return [(src, D_ROPE), (None, LANE - D_ROPE)]


def _wprep_t_kernel(src_ref, nv_ref, wt_ref, g_ref, o_ref):
    j = pl.program_id(0)
    x = wt_ref[...] * g_ref[...]
    keep = lax.broadcasted_iota(jnp.int32, x.shape, 0) < nv_ref[j]
    o_ref[...] = jnp.where(keep, x, 0.0).T.astype(BF16)


def _wprep_t(wt, layer, g, src_rows, n_valid, name):
    _, n, k = wt.shape
    unit = 2 * SUBLANES
    assert max(src_rows) + LANE <= n and all(r % unit == 0 for r in src_rows)
    nblk = len(src_rows)
    src_rows = [r // unit for r in src_rows]
    grid_spec = pltpu.PrefetchScalarGridSpec(
        num_scalar_prefetch=2,
        grid=(nblk,),
        in_specs=[pl.BlockSpec((pl.Squeezed(), pl.Element(LANE), pl.Element(k)),
                               lambda j, src, nv: (layer, src[j] * unit, 0)),
                  pl.BlockSpec((1, k), lambda j, src, nv: (0, 0))],
        out_specs=pl.BlockSpec((k, LANE), lambda j, src, nv: (0, j)),
    )
    return pl.pallas_call(
        _wprep_t_kernel,
        out_shape=jax.ShapeDtypeStruct((k, nblk * LANE), BF16),
        grid_spec=grid_spec,
        compiler_params=_params(("arbitrary",), VMEM_LIMIT),
        name=name,
    )(jnp.asarray(src_rows, jnp.int32), jnp.asarray(n_valid, jnp.int32), wt, g.reshape(1, k))


def _even_w_in(w, layer, g):
    src = dict(zip(EVEN_NAMES, [int(c) for c in np.cumsum((0,) + EVEN_SPLITS[:-1])]))
    width = dict(zip(EVEN_NAMES, EVEN_SPLITS))
    rows, valid = [], []
    for name in EVEN_ORDER:
        for off in range(0, width[name], LANE):
            rows.append(src[name] + off)
            valid.append(min(LANE, width[name] - off))
    return _wprep_t(jnp.swapaxes(w, 1, 2), layer, g, rows, valid, "even_w_prep")


def _odd_w_in(w, layer, g):
    cu, cv, gc, cq, ckv, krope, gd = [int(c) for c in np.cumsum((0,) + ODD_SPLITS[:-1])]
    wt = jnp.swapaxes(w, 1, 2)
    rows = list(range(cu, krope, LANE)) + list(range(gd, gd + D_WIDTH, LANE))
    main = _wprep_t(wt, layer, g, rows, [LANE] * len(rows), "odd_w_prep")
    w_kr = _wprep_t(wt, layer, g, [krope], [D_ROPE], "rope_w_prep")
    return main, w_kr


def _w_uq_layout(w, layer, g):
    plan = []
    for h in range(D_HEADS):
        plan.append((h * D_HEAD_PAD, [(h * D_QK, D_NOPE)]))
        plan.append((h * D_HEAD_PAD + D_NOPE, _rope_pieces(h * D_QK + D_NOPE)))
    return _wprep(w, layer, g, [plan], [D_HEADS * D_HEAD_PAD], "uq_w_prep")[0]


def _w_ukv_layout(w, layer, g):
    plan = []
    for h in range(D_HEADS):
        plan.append((h * D_NOPE, [(h * (D_NOPE + D_V), D_NOPE)]))
        plan.append((D_HEADS * D_NOPE + h * D_V, [(h * (D_NOPE + D_V) + D_NOPE, D_V)]))
    return _wprep(w, layer, g, [plan], [D_HEADS * (D_NOPE + D_V)], "ukv_w_prep")[0]


def _even_mixer(xb, pos_f, g_in, w_in, layer, a_qn, a_kn, b_gn, batch, seq):
    z = _norm_proj(xb, _even_w_in(w_in, layer, g_in), 1024, 512, name="even_in")
    a = _dsa(z, pos_f, a_qn.reshape(1, LANE), a_kn.reshape(1, LANE), batch, seq)
    b = _retention(z, b_gn, batch, seq)
    return a, b


def _odd_mixer(xb, pos_f, g_in, w_in, layer, c_vn, c_ws, c_bs, d_qlg, d_kvlg, d_wuq, d_wukv, d_qn, d_kn, batch, seq):
    w_main, w_kr = _odd_w_in(w_in, layer, g_in)
    z = _norm_proj(xb, w_main, 1024, 1024, name="odd_in")
    kr = _norm_proj(xb, w_kr, 1024, LANE, name="odd_in_rope")
    c = _gmlp(z, c_vn, c_ws, c_bs)
    cq_block = 3 * C_WIDTH // D_Q_LORA
    ckv_block = (3 * C_WIDTH + D_Q_LORA) // D_KV_LORA
    q_raw = _norm_proj(z, _w_uq_layout(d_wuq, layer, d_qlg), 1024, 1024, col_block=cq_block, name="mla_uq")
    kv = _norm_proj(z, _w_ukv_layout(d_wukv, layer, d_kvlg), 1024, 1024, col_block=ckv_block, name="mla_ukv")
    q, k = _mla_prep(q_raw, kv, kr, pos_f, d_qn, d_kn)
    d = _mla_attn(q, k, kv, z, batch, seq)
    return c, d


def kernel(x, p, positions, norm_in, even_w_in, even_a_q_norm, even_a_k_norm, even_b_group_norm, even_w_out, odd_w_in, odd_c_v_norm, odd_c_w_s, odd_c_b_s, odd_d_q_lora_norm, odd_d_kv_lora_norm, odd_d_w_uq, odd_d_w_ukv, odd_d_q_norm, odd_d_k_norm, odd_w_out, ple_w_in, ple_w_gate):
    batch, seq, d = x.shape
    t = batch * seq
    depth = norm_in.shape[0]
    xs = x.reshape(t, d)
    xb = xs
    pos_f = positions.astype(F32)
    p2 = p.reshape(depth, t, PLE_DIM)
    w_out = (even_w_out.astype(BF16), odd_w_out.astype(BF16))
    w_gate = ple_w_gate.astype(BF16)
    w_pin = ple_w_in.astype(BF16)
    for i in range(depth):
        j = i // 2
        if i % 2 == 0:
            m1, m2 = _even_mixer(xb, pos_f, norm_in[i], even_w_in, j, even_a_q_norm[j], even_a_k_norm[j],
                                 even_b_group_norm[j], batch, seq)
        else:
            m1, m2 = _odd_mixer(xb, pos_f, norm_in[i], odd_w_in, j, odd_c_v_norm[j], odd_c_w_s[j], odd_c_b_s[j],
                                odd_d_q_lora_norm[j], odd_d_kv_lora_norm[j], odd_d_w_uq, odd_d_w_ukv,
                                odd_d_q_norm[j], odd_d_k_norm[j], batch, seq)
        xs, xb = _out_proj(m1, m2, w_out[i % 2], j, xs, 1024, 512)
        xs, xb = _ple(xb, w_gate, p2, w_pin, i, xs, 1024, 512)
    return xs.reshape(batch, seq, d)
```
